```python
import math
import jax, jax.numpy as jnp
from jax import lax
import numpy as np

D_MODEL = 1024
BATCH = 16
SEQ = 4096
DEPTH = 1

D_MIX = D_MODEL
MLA_HEADS = 4
MLA_NOPE = 128
MLA_ROPE = 64
MLA_QK = MLA_NOPE + MLA_ROPE
MLA_V = 128
MLA_Q_LORA = 384
MLA_KV_LORA = 256
DIFF_HEADS = 4
DIFF_DQK = 64
DIFF_DV = 2 * DIFF_DQK
IN_MLA = MLA_Q_LORA + MLA_KV_LORA + MLA_ROPE
IN_DIFF_Q = DIFF_HEADS * 2 * DIFF_DQK
IN_DIFF_V = DIFF_HEADS * DIFF_DV
IN_COLS = IN_MLA + 2 * IN_DIFF_Q + IN_DIFF_V
ROPE_THETA = 10000.0
Q_BLOCK = 128
RMS_EPS = 1e-6
PEER_HEADS = 8
PEER_NKEYS = 128
PEER_EXPERTS = PEER_NKEYS * PEER_NKEYS
PEER_DK_HALF = 128
PEER_TOPK = 16
PEER_CHUNK = 128
ADA_INIT = 0.2

kernel_name = "hybrid_mla_diffattn_peer_adaln"


def rms_norm(x, g):
    xf = x.astype(jnp.float32)
    y = xf * lax.rsqrt(jnp.mean(xf * xf, axis=-1, keepdims=True) + RMS_EPS)
    return (y * g.astype(jnp.float32)).astype(x.dtype)


def rope_tables(seq, dim):
    half = dim // 2
    inv = 1.0 / (ROPE_THETA ** (jnp.arange(half, dtype=jnp.float32) / half))
    ang = jnp.arange(seq, dtype=jnp.float32)[:, None] * inv[None, :]
    return jnp.cos(ang), jnp.sin(ang)


def apply_rope(x, cos, sin):
    half = x.shape[-1] // 2
    xf = x.astype(jnp.float32)
    x1, x2 = xf[..., :half], xf[..., half:]
    return jnp.concatenate([x1 * cos - x2 * sin, x2 * cos + x1 * sin], axis=-1).astype(x.dtype)


def causal_attention(qs, ks, coeffs, v, scale):
    B, H, S, dv = v.shape
    nb = S // Q_BLOCK
    qbs = tuple(q.reshape(B, H, nb, Q_BLOCK, q.shape[-1]).transpose(2, 0, 1, 3, 4) for q in qs)
    kpos = jnp.arange(S)

    def one_block(args):
        i, qb = args
        qpos = i * Q_BLOCK + jnp.arange(Q_BLOCK)
        mask = kpos[None, :] <= qpos[:, None]
        w = None
        for qi, ki, ci in zip(qb, ks, coeffs):
            s = jnp.einsum('bhqd,bhkd->bhqk', qi, ki).astype(jnp.float32) * scale
            p = jax.nn.softmax(jnp.where(mask, s, -jnp.inf), axis=-1)
            w = ci * p if w is None else w + ci * p
        return jnp.einsum('bhqk,bhkd->bhqd', w.astype(v.dtype), v)

    out = lax.map(one_block, (jnp.arange(nb), qbs))
    return out.transpose(1, 2, 0, 3, 4).reshape(B, H, S, dv)


def peer_ffn(h, w_q, sub_keys, u_tab, v_tab):
    T, D = h.shape
    hc = h.reshape(T // PEER_CHUNK, PEER_CHUNK, D)

    def one_chunk(hb):
        C = hb.shape[0]
        q = (hb @ w_q).reshape(C, PEER_HEADS, 2, PEER_DK_HALF)
        s = jnp.einsum('thpd,hpnd->thpn', q, sub_keys).astype(jnp.float32)
        sv, si = lax.top_k(s, PEER_TOPK)
        cand = sv[:, :, 0, :, None] + sv[:, :, 1, None, :]
        cidx = si[:, :, 0, :, None] * PEER_NKEYS + si[:, :, 1, None, :]
        cand = cand.reshape(C, PEER_HEADS, PEER_TOPK * PEER_TOPK)
        cidx = cidx.reshape(C, PEER_HEADS, PEER_TOPK * PEER_TOPK)
        fv, fpos = lax.top_k(cand, PEER_TOPK)
        eidx = jnp.take_along_axis(cidx, fpos, axis=-1).reshape(C, PEER_HEADS * PEER_TOPK)
        g = jax.nn.softmax(fv, axis=-1).reshape(C, PEER_HEADS * PEER_TOPK)
        u = u_tab[eidx]
        a = jax.nn.gelu(jnp.einsum('ted,td->te', u, hb).astype(jnp.float32), approximate=False)
        coef = (g * a).astype(hb.dtype)
        vv = v_tab[eidx]
        return jnp.einsum('te,ted->td', coef, vv)

    return lax.map(one_chunk, hc).reshape(T, D)


def setup_inputs(seed: int = 0) -> dict:
    key = jax.random.key(seed)
    keys = iter(jax.random.split(key, 40))

    def nrm(shape, scale):
        return jax.random.normal(next(keys), shape, jnp.float32) * scale

    def gain(shape):
        return 1.0 + nrm(shape, 0.02)

    L = DEPTH
    return {
        'x': nrm((BATCH, SEQ, D_MODEL), 1.0),
        'c': nrm((BATCH, D_MODEL), 1.0),
        'ada_w': nrm((L, D_MODEL, 6 * D_MODEL), ADA_INIT * D_MODEL ** -0.5),
        'ada_b': nrm((L, 6 * D_MODEL), 0.02),
        'norm1_g': gain((L, D_MODEL)),
        'w_in': nrm((L, D_MODEL, IN_COLS), D_MODEL ** -0.5),
        'mla_q_lat_g': gain((L, MLA_Q_LORA)),
        'mla_w_q_up': nrm((L, MLA_Q_LORA, MLA_HEADS * MLA_QK), MLA_Q_LORA ** -0.5),
        'mla_kv_lat_g': gain((L, MLA_KV_LORA)),
        'mla_w_kv_up': nrm((L, MLA_KV_LORA, MLA_HEADS * (MLA_NOPE + MLA_V)), MLA_KV_LORA ** -0.5),
        'mla_q_g': gain((L, MLA_QK)),
        'mla_k_g': gain((L, MLA_QK)),
        'diff_q_g': gain((L, DIFF_DQK)),
        'diff_k_g': gain((L, DIFF_DQK)),
        'diff_lq1': nrm((L, DIFF_DQK), 0.1),
        'diff_lk1': nrm((L, DIFF_DQK), 0.1),
        'diff_lq2': nrm((L, DIFF_DQK), 0.1),
        'diff_lk2': nrm((L, DIFF_DQK), 0.1),
        'diff_subln_g': gain((L, DIFF_DV)),
        'w_out': nrm((L, D_MIX, D_MODEL), D_MIX ** -0.5),
        'norm2_g': gain((L, D_MODEL)),
        'peer_w_q': nrm((L, D_MODEL, PEER_HEADS * 2 * PEER_DK_HALF), D_MODEL ** -0.5),
        'peer_sub_keys': nrm((L, PEER_HEADS, 2, PEER_NKEYS, PEER_DK_HALF), PEER_DK_HALF ** -0.5),
        'peer_u': nrm((L, PEER_EXPERTS, D_MODEL), D_MODEL ** -0.5),
        'peer_v': nrm((L, PEER_EXPERTS, D_MODEL), PEER_HEADS ** -0.5),
    }


def reference(x, c, ada_w, ada_b, norm1_g, w_in, mla_q_lat_g, mla_w_q_up, mla_kv_lat_g,
              mla_w_kv_up, mla_q_g, mla_k_g, diff_q_g, diff_k_g, diff_lq1, diff_lk1, diff_lq2,
              diff_lk2, diff_subln_g, w_out, norm2_g, peer_w_q, peer_sub_keys, peer_u, peer_v):
    B, S, D = x.shape
    cos_m, sin_m = rope_tables(S, MLA_ROPE)
    cos_d, sin_d = rope_tables(S, DIFF_DQK)

    for l in range(DEPTH):
        mod = jax.nn.silu(c) @ ada_w[l] + ada_b[l]
        sh1, sc1, g1, sh2, sc2, g2 = [m[:, None, :] for m in jnp.split(mod, 6, axis=-1)]

        h = rms_norm(x, norm1_g[l]) * (1.0 + sc1) + sh1
        proj = h @ w_in[l]
        o0 = MLA_Q_LORA
        o1 = o0 + MLA_KV_LORA
        o2 = o1 + MLA_ROPE
        o3 = o2 + IN_DIFF_Q
        o4 = o3 + IN_DIFF_Q
        q_lat, kv_lat, kpe_raw = proj[..., :o0], proj[..., o0:o1], proj[..., o1:o2]
        dq, dk, dv = proj[..., o2:o3], proj[..., o3:o4], proj[..., o4:]

        q = (rms_norm(q_lat, mla_q_lat_g[l]) @ mla_w_q_up[l]).reshape(B, S, MLA_HEADS, MLA_QK)
        q = q.transpose(0, 2, 1, 3)
        q_nope = rms_norm(q[..., :MLA_NOPE], mla_q_g[l, :MLA_NOPE])
        q_pe = apply_rope(rms_norm(q[..., MLA_NOPE:], mla_q_g[l, MLA_NOPE:]), cos_m, sin_m)
        kv = (rms_norm(kv_lat, mla_kv_lat_g[l]) @ mla_w_kv_up[l]).reshape(B, S, MLA_HEADS, MLA_NOPE + MLA_V)
        kv = kv.transpose(0, 2, 1, 3)
        k_nope = rms_norm(kv[..., :MLA_NOPE], mla_k_g[l, :MLA_NOPE])
        v_m = kv[..., MLA_NOPE:]
        k_pe = apply_rope(rms_norm(kpe_raw, mla_k_g[l, MLA_NOPE:])[:, None], cos_m, sin_m)
        k_m = jnp.concatenate([k_nope, jnp.broadcast_to(k_pe, (B, MLA_HEADS, S, MLA_ROPE))], axis=-1)
        q_m = jnp.concatenate([q_nope, q_pe], axis=-1)
        out_m = causal_attention((q_m,), (k_m,), (1.0,), v_m, MLA_QK ** -0.5)

        qd = dq.reshape(B, S, DIFF_HEADS, 2, DIFF_DQK).transpose(0, 2, 3, 1, 4)
        kd = dk.reshape(B, S, DIFF_HEADS, 2, DIFF_DQK).transpose(0, 2, 3, 1, 4)
        qd = apply_rope(rms_norm(qd, diff_q_g[l]), cos_d, sin_d)
        kd = apply_rope(rms_norm(kd, diff_k_g[l]), cos_d, sin_d)
        vd = dv.reshape(B, S, DIFF_HEADS, DIFF_DV).transpose(0, 2, 1, 3)
        lambda_init = 0.8 - 0.6 * math.exp(-0.3 * l)
        lam = (jnp.exp(jnp.sum(diff_lq1[l].astype(jnp.float32) * diff_lk1[l].astype(jnp.float32)))
               - jnp.exp(jnp.sum(diff_lq2[l].astype(jnp.float32) * diff_lk2[l].astype(jnp.float32)))
               + lambda_init)
        out_d = causal_attention((qd[:, :, 0], qd[:, :, 1]), (kd[:, :, 0], kd[:, :, 1]),
                                 (1.0, -lam), vd, DIFF_DQK ** -0.5)
        out_d = rms_norm(out_d, diff_subln_g[l]) * (1.0 - lambda_init)

        mixed = jnp.concatenate([out_m.transpose(0, 2, 1, 3).reshape(B, S, MLA_HEADS * MLA_V),
                                 out_d.transpose(0, 2, 1, 3).reshape(B, S, DIFF_HEADS * DIFF_DV)], axis=-1)
        x = x + g1 * (mixed @ w_out[l])

        h2 = rms_norm(x, norm2_g[l]) * (1.0 + sc2) + sh2
        y = peer_ffn(h2.reshape(B * S, D), peer_w_q[l], peer_sub_keys[l], peer_u[l], peer_v[l])
        x = x + g2 * y.reshape(B, S, D)
    return x
```

```python
import functools
import math

import jax
import jax.numpy as jnp
import numpy as np
from jax import lax
from jax.experimental import pallas as pl
from jax.experimental.pallas import tpu as pltpu

F32 = jnp.float32
BF16 = jnp.bfloat16

MLA_HEADS = 4
MLA_NOPE = 128
MLA_ROPE = 64
MLA_QK = MLA_NOPE + MLA_ROPE
MLA_V = 128
MLA_Q_LORA = 384
MLA_KV_LORA = 256
DIFF_HEADS = 4
DIFF_DQK = 64
DIFF_DV = 128
ROPE_THETA = 10000.0
RMS_EPS = 1e-6
PEER_HEADS = 8
PEER_NKEYS = 128
PEER_DK_HALF = 128
PEER_TOPK = 16
NEG_INF = float("-inf")

LANES = 128
VMEM_LIMIT_BYTES = 56 * 1024 * 1024

PROJ_TS = 512
ATT_TQ = 256
ATT_TK = 512
ROUTE_TS = 256
PEER_TT = 128
PEER_NSLOT = 4
PEER_NE = PEER_HEADS * PEER_TOPK


def _rms(x, n):
    ss = jnp.sum(x * x, axis=-1, keepdims=True)
    return x * lax.rsqrt(ss * (1.0 / n) + RMS_EPS)


def _swap_halves64(x):
    lane = lax.broadcasted_iota(jnp.int32, x.shape, x.ndim - 1)
    first = (lane % 64) < 32
    return jnp.where(first, pltpu.roll(x, 96, x.ndim - 1), pltpu.roll(x, 32, x.ndim - 1))


def _rope(x, cosf, sinf):
    return x * cosf + _swap_halves64(x) * sinf


def _ada_kernel(c_ref, w_ref, b_ref, lq1_ref, lk1_ref, lq2_ref, lk2_ref, mod_ref, lam_ref, *, lambda_init):
    c = c_ref[...]
    s = c * jax.nn.sigmoid(c)
    mod_ref[...] = jnp.dot(s, w_ref[...], preferred_element_type=F32) + b_ref[...]
    d1 = jnp.sum(lq1_ref[...] * lk1_ref[...], axis=-1, keepdims=True)
    d2 = jnp.sum(lq2_ref[...] * lk2_ref[...], axis=-1, keepdims=True)
    lam = jnp.exp(d1) - jnp.exp(d2) + lambda_init
    lam_ref[...] = jnp.broadcast_to(lam, lam_ref.shape)


def _ada(c, ada_w, ada_b, lq1, lk1, lq2, lk2, lambda_init):
    B, D = c.shape
    N = ada_w.shape[1]
    bn = 1024
    small = pl.BlockSpec((1, DIFF_DQK), lambda j: (0, 0))
    return pl.pallas_call(
        functools.partial(_ada_kernel, lambda_init=lambda_init),
        grid=(N // bn,),
        in_specs=[
            pl.BlockSpec((B, D), lambda j: (0, 0)),
            pl.BlockSpec((D, bn), lambda j: (0, j)),
            pl.BlockSpec((1, bn), lambda j: (0, j)),
            small, small, small, small,
        ],
        out_specs=[pl.BlockSpec((B, bn), lambda j: (0, j)), pl.BlockSpec((1, LANES), lambda j: (0, 0))],
        out_shape=[jax.ShapeDtypeStruct((B, N), F32), jax.ShapeDtypeStruct((1, LANES), F32)],
        compiler_params=pltpu.CompilerParams(dimension_semantics=("arbitrary",)),
        name="ada",
    )(c, ada_w, ada_b.reshape(1, N), lq1.reshape(1, -1), lk1.reshape(1, -1), lq2.reshape(1, -1), lk2.reshape(1, -1))


C_QLAT = 0
C_KVLAT = C_QLAT + MLA_Q_LORA
C_DQ = C_KVLAT + MLA_KV_LORA
C_DK = C_DQ + DIFF_HEADS * 2 * DIFF_DQK
C_DV = C_DK + DIFF_HEADS * 2 * DIFF_DQK
C_KPE = C_DV + DIFF_HEADS * DIFF_DV
C_END = C_KPE + LANES


def _proj_kernel(x_ref, sc_ref, sh_ref, n1g_ref, win_ref, qlg_ref, wq_ref, kvlg_ref, wkv_ref,
                 qgn_ref, qgr_ref, kgn_ref, kgr_ref, dqg_ref, dkg_ref, cos_ref, sin_ref,
                 qm_ref, km_ref, vm_ref, qd_ref, kd_ref, vd_ref):
    x = x_ref[0]
    D = x.shape[-1]
    h = _rms(x, D) * n1g_ref[...] * (1.0 + sc_ref[0, 0]) + sh_ref[0, 0]
    proj = jnp.dot(h.astype(BF16), win_ref[...], preferred_element_type=F32)
    cosf = cos_ref[...]
    sinf = sin_ref[...]

    q_lat = proj[:, C_QLAT:C_QLAT + MLA_Q_LORA]
    q = jnp.dot((_rms(q_lat, MLA_Q_LORA) * qlg_ref[...]).astype(BF16), wq_ref[...], preferred_element_type=F32)
    kv_lat = proj[:, C_KVLAT:C_KVLAT + MLA_KV_LORA]
    kv = jnp.dot((_rms(kv_lat, MLA_KV_LORA) * kvlg_ref[...]).astype(BF16), wkv_ref[...], preferred_element_type=F32)
    kpe = _rope(_rms(proj[:, C_KPE:C_KPE + LANES], MLA_ROPE) * kgr_ref[...], cosf, sinf).astype(BF16)
    q_scale = MLA_QK ** -0.5
    for hd in range(MLA_HEADS):
        qn = _rms(q[:, hd * 256:hd * 256 + 128], MLA_NOPE) * qgn_ref[...]
        qr = _rope(_rms(q[:, hd * 256 + 128:hd * 256 + 256], MLA_ROPE) * qgr_ref[...], cosf, sinf)
        qm_ref[0, hd, :, 0:128] = (qn * q_scale).astype(BF16)
        qm_ref[0, hd, :, 128:256] = (qr * q_scale).astype(BF16)
        kn = _rms(kv[:, hd * 256:hd * 256 + 128], MLA_NOPE) * kgn_ref[...]
        km_ref[0, hd, :, 0:128] = kn.astype(BF16)
        km_ref[0, hd, :, 128:256] = kpe
        vm_ref[0, hd] = kv[:, hd * 256 + 128:hd * 256 + 256].astype(BF16)

    lane = lax.broadcasted_iota(jnp.int32, (x.shape[0], LANES), 1)
    first = lane < DIFF_DQK

    def seg_norm(t):
        t2 = t * t
        s1 = jnp.sum(jnp.where(first, t2, 0.0), axis=-1, keepdims=True)
        s2 = jnp.sum(jnp.where(first, 0.0, t2), axis=-1, keepdims=True)
        r = jnp.where(first, lax.rsqrt(s1 * (1.0 / DIFF_DQK) + RMS_EPS), lax.rsqrt(s2 * (1.0 / DIFF_DQK) + RMS_EPS))
        return t * r

    d_scale = DIFF_DQK ** -0.5
    for hd in range(DIFF_HEADS):
        tq = _rope(seg_norm(proj[:, C_DQ + hd * 128:C_DQ + (hd + 1) * 128]) * dqg_ref[...], cosf, sinf) * d_scale
        qd_ref[0, hd, 0] = jnp.where(first, tq, 0.0).astype(BF16)
        qd_ref[0, hd, 1] = jnp.where(first, 0.0, tq).astype(BF16)
        tk = _rope(seg_norm(proj[:, C_DK + hd * 128:C_DK + (hd + 1) * 128]) * dkg_ref[...], cosf, sinf)
        kd_ref[0, hd] = tk.astype(BF16)
        vd_ref[0, hd] = proj[:, C_DV + hd * 128:C_DV + (hd + 1) * 128].astype(BF16)


def _proj(x, mod4, n1g, win_p, qlg, wq_p, kvlg, wkv, qgn, qgr, kgn, kgr, dqg, dkg, cosf, sinf):
    B, S, D = x.shape
    ts = min(PROJ_TS, S)
    row = lambda n: pl.BlockSpec((1, n), lambda b, s: (0, 0))
    full = lambda a: pl.BlockSpec(a.shape, lambda b, s: (0, 0))
    head_out = lambda w: pl.BlockSpec((1, MLA_HEADS, ts, w), lambda b, s: (b, 0, s, 0))
    return pl.pallas_call(
        _proj_kernel,
        grid=(B, S // ts),
        in_specs=[
            pl.BlockSpec((1, ts, D), lambda b, s: (b, s, 0)),
            pl.BlockSpec((1, 1, 1, D), lambda b, s: (b, 1, 0, 0)),
            pl.BlockSpec((1, 1, 1, D), lambda b, s: (b, 0, 0, 0)),
            row(D), full(win_p), row(MLA_Q_LORA), full(wq_p), row(MLA_KV_LORA), full(wkv),
            row(LANES), row(LANES), row(LANES), row(LANES), row(LANES), row(LANES),
            pl.BlockSpec((ts, LANES), lambda b, s: (s, 0)),
            pl.BlockSpec((ts, LANES), lambda b, s: (s, 0)),
        ],
        out_specs=[
            head_out(256), head_out(256), head_out(128),
            pl.BlockSpec((1, DIFF_HEADS, 2, ts, LANES), lambda b, s: (b, 0, 0, s, 0)),
            head_out(128), head_out(128),
        ],
        out_shape=[
            jax.ShapeDtypeStruct((B, MLA_HEADS, S, 256), BF16),
            jax.ShapeDtypeStruct((B, MLA_HEADS, S, 256), BF16),
            jax.ShapeDtypeStruct((B, MLA_HEADS, S, MLA_V), BF16),
            jax.ShapeDtypeStruct((B, DIFF_HEADS, 2, S, LANES), BF16),
            jax.ShapeDtypeStruct((B, DIFF_HEADS, S, LANES), BF16),
            jax.ShapeDtypeStruct((B, DIFF_HEADS, S, DIFF_DV), BF16),
        ],
        compiler_params=pltpu.CompilerParams(dimension_semantics=("arbitrary", "arbitrary"),
                                             vmem_limit_bytes=VMEM_LIMIT_BYTES),
        name="proj",
    )(x, mod4, mod4, n1g, win_p, qlg, wq_p, kvlg, wkv, qgn, qgr, kgn, kgr, dqg, dkg, cosf, sinf)


def _online_step(s, v, m, l, acc):
    m_new = jnp.maximum(m, jnp.max(s, axis=-1, keepdims=True))
    p = jnp.exp(s - m_new)
    alpha = jnp.exp(m - m_new)
    l = alpha * l + jnp.sum(p, axis=-1, keepdims=True)
    acc = alpha * acc + jnp.dot(p.astype(BF16), v, preferred_element_type=F32)
    return m_new, l, acc


def _causal_mask(qi, j, tq, tk):
    qpos = qi * tq + lax.broadcasted_iota(jnp.int32, (tq, tk), 0)
    kpos = j * tk + lax.broadcasted_iota(jnp.int32, (tq, tk), 1)
    return kpos <= qpos


_NT = (((1,), (1,)), ((), ()))


def _mla_kernel(q_ref, k_ref, v_ref, o_ref, *, tq, tk):
    qi = pl.program_id(2)
    q = q_ref[0, 0]
    nkv = ((qi + 1) * tq + tk - 1) // tk

    def body(j, carry):
        m, l, acc = carry
        off = pl.multiple_of(j * tk, tk)
        k = k_ref[0, 0, pl.ds(off, tk), :]
        v = v_ref[0, 0, pl.ds(off, tk), :]
        s = lax.dot_general(q, k, _NT, preferred_element_type=F32)
        s = jnp.where(_causal_mask(qi, j, tq, tk), s, NEG_INF)
        return _online_step(s, v, m, l, acc)

    init = (jnp.full((tq, 1), NEG_INF, F32), jnp.zeros((tq, 1), F32), jnp.zeros((tq, MLA_V), F32))
    m, l, acc = lax.fori_loop(0, nkv, body, init)
    o_ref[0] = (acc / l).astype(o_ref.dtype)


def _diff_kernel(lam_ref, g_ref, q_ref, k_ref, v_ref, o_ref, *, tq, tk, out_scale):
    qi = pl.program_id(2)
    q1 = q_ref[0, 0, 0]
    q2 = q_ref[0, 0, 1]
    nkv = ((qi + 1) * tq + tk - 1) // tk

    def body(j, carry):
        m1, l1, a1, m2, l2, a2 = carry
        off = pl.multiple_of(j * tk, tk)
        k = k_ref[0, 0, pl.ds(off, tk), :]
        v = v_ref[0, 0, pl.ds(off, tk), :]
        mask = _causal_mask(qi, j, tq, tk)
        s1 = jnp.where(mask, lax.dot_general(q1, k, _NT, preferred_element_type=F32), NEG_INF)
        m1, l1, a1 = _online_step(s1, v, m1, l1, a1)
        s2 = jnp.where(mask, lax.dot_general(q2, k, _NT, preferred_element_type=F32), NEG_INF)
        m2, l2, a2 = _online_step(s2, v, m2, l2, a2)
        return m1, l1, a1, m2, l2, a2

    one = (jnp.full((tq, 1), NEG_INF, F32), jnp.zeros((tq, 1), F32), jnp.zeros((tq, DIFF_DV), F32))
    m1, l1, a1, m2, l2, a2 = lax.fori_loop(0, nkv, body, one + one)
    o = a1 / l1 - lam_ref[...] * (a2 / l2)
    o = _rms(o, DIFF_DV) * g_ref[...] * out_scale
    o_ref[0] = o.astype(o_ref.dtype)


def _mla_attention(qm, km, vm):
    B, H, S, _ = qm.shape
    tq, tk = min(ATT_TQ, S), min(ATT_TK, S)
    return pl.pallas_call(
        functools.partial(_mla_kernel, tq=tq, tk=tk),
        grid=(B, H, S // tq),
        in_specs=[
            pl.BlockSpec((1, 1, tq, 256), lambda b, h, i: (b, h, i, 0)),
            pl.BlockSpec((1, 1, S, 256), lambda b, h, i: (b, h, 0, 0)),
            pl.BlockSpec((1, 1, S, MLA_V), lambda b, h, i: (b, h, 0, 0)),
        ],
        out_specs=pl.BlockSpec((1, tq, MLA_V), lambda b, h, i: (b, i, h)),
        out_shape=jax.ShapeDtypeStruct((B, S, H * MLA_V), BF16),
        compiler_params=pltpu.CompilerParams(dimension_semantics=("arbitrary",) * 3,
                                             vmem_limit_bytes=VMEM_LIMIT_BYTES),
        name="mla_attn",
    )(qm, km, vm)


def _diff_attention(lam, subln_g, qd, kd, vd, out_scale):
    B, H, _, S, _ = qd.shape
    tq, tk = min(ATT_TQ, S), min(ATT_TK, S)
    return pl.pallas_call(
        functools.partial(_diff_kernel, tq=tq, tk=tk, out_scale=out_scale),
        grid=(B, H, S // tq),
        in_specs=[
            pl.BlockSpec((1, LANES), lambda b, h, i: (0, 0)),
            pl.BlockSpec((1, DIFF_DV), lambda b, h, i: (0, 0)),
            pl.BlockSpec((1, 1, 2, tq, LANES), lambda b, h, i: (b, h, 0, i, 0)),
            pl.BlockSpec((1, 1, S, LANES), lambda b, h, i: (b, h, 0, 0)),
            pl.BlockSpec((1, 1, S, DIFF_DV), lambda b, h, i: (b, h, 0, 0)),
        ],
        out_specs=pl.BlockSpec((1, tq, DIFF_DV), lambda b, h, i: (b, i, h)),
        out_shape=jax.ShapeDtypeStruct((B, S, H * DIFF_DV), BF16),
        compiler_params=pltpu.CompilerParams(dimension_semantics=("arbitrary",) * 3,
                                             vmem_limit_bytes=VMEM_LIMIT_BYTES),
        name="diff_attn",
    )(lam, subln_g, qd, kd, vd)


def _topk_rows(vals, k, extra=None):
    n = vals.shape[0]
    row = lax.broadcasted_iota(jnp.int32, vals.shape, 0)
    out_v, out_i, out_e = [], [], []
    for _ in range(k):
        m = jnp.max(vals, axis=0, keepdims=True)
        sel = jnp.min(jnp.where(vals == m, row, n), axis=0, keepdims=True)
        hit = row == sel
        out_v.append(m)
        out_i.append(sel)
        if extra is not None:
            out_e.append(jnp.max(jnp.where(hit, extra, -1), axis=0, keepdims=True))
        vals = jnp.where(hit, NEG_INF, vals)
    cat = lambda xs: jnp.concatenate(xs, axis=0)
    return cat(out_v), cat(out_i), (cat(out_e) if extra is not None else None)


def _route_kernel(x_ref, mm_ref, md_ref, woa_ref, wob_ref, g1_ref, sc_ref, sh_ref, n2g_ref, wq_ref, keys_ref,
                  x1_ref, h2_ref, eidx_ref, gw_ref, st_scr, sv_scr, si_scr, gt_scr):
    x = x_ref[0]
    D = x.shape[-1]
    o = jnp.dot(mm_ref[0], woa_ref[...], preferred_element_type=F32)
    o = o + jnp.dot(md_ref[0], wob_ref[...], preferred_element_type=F32)
    x1 = x + g1_ref[0, 0] * o
    x1_ref[0] = x1
    h2 = _rms(x1, D) * n2g_ref[...] * (1.0 + sc_ref[0, 0]) + sh_ref[0, 0]
    h2_ref[0] = h2
    q = jnp.dot(h2.astype(BF16), wq_ref[...], preferred_element_type=F32).astype(BF16)
    ngroups = 2 * PEER_HEADS
    for g in range(ngroups):
        st_scr[g] = lax.dot_general(keys_ref[g], q[:, g * PEER_DK_HALF:(g + 1) * PEER_DK_HALF], _NT,
                                    preferred_element_type=F32)

    def sub_topk(g, carry):
        v, i, _ = _topk_rows(st_scr[g], PEER_TOPK)
        sv_scr[g] = v
        si_scr[g] = i
        return carry

    lax.fori_loop(0, ngroups, sub_topk, 0)

    def head_topk(hd, carry):
        v0, v1 = sv_scr[2 * hd], sv_scr[2 * hd + 1]
        i0, i1 = si_scr[2 * hd], si_scr[2 * hd + 1]
        cand = jnp.concatenate([v0[a:a + 1] + v1 for a in range(PEER_TOPK)], axis=0)
        cidx = jnp.concatenate([i0[a:a + 1] * PEER_NKEYS + i1 for a in range(PEER_TOPK)], axis=0)
        fv, _, e = _topk_rows(cand, PEER_TOPK, extra=cidx)
        p = jnp.exp(fv - fv[0:1])
        gate = p / jnp.sum(p, axis=0, keepdims=True)
        r0 = pl.multiple_of(hd * PEER_TOPK, PEER_TOPK)
        eidx_ref[pl.ds(r0, PEER_TOPK), :] = e
        gt_scr[pl.ds(r0, PEER_TOPK), :] = gate
        return carry

    lax.fori_loop(0, PEER_HEADS, head_topk, 0)
    gw_ref[...] = gt_scr[...].T


def _route(x, mixed_m, mixed_d, woa, wob, mod4, n2g, wq, keys):
    B, S, D = x.shape
    T = B * S
    ts = min(ROUTE_TS, S)
    nst = S // ts
    full = lambda a: pl.BlockSpec(a.shape, lambda b, s: (0,) * a.ndim)
    modspec = lambda k: pl.BlockSpec((1, 1, 1, D), lambda b, s: (b, k, 0, 0))
    tok = lambda w: pl.BlockSpec((1, ts, w), lambda b, s: (b, s, 0))
    return pl.pallas_call(
        _route_kernel,
        grid=(B, nst),
        in_specs=[
            tok(D), tok(mixed_m.shape[-1]), tok(mixed_d.shape[-1]), full(woa), full(wob),
            modspec(2), modspec(4), modspec(3),
            pl.BlockSpec((1, D), lambda b, s: (0, 0)), full(wq), full(keys),
        ],
        out_specs=[
            tok(D), tok(D),
            pl.BlockSpec((PEER_NE, ts), lambda b, s: (0, b * nst + s)),
            pl.BlockSpec((ts, PEER_NE), lambda b, s: (b * nst + s, 0)),
        ],
        out_shape=[
            jax.ShapeDtypeStruct((B, S, D), F32),
            jax.ShapeDtypeStruct((B, S, D), F32),
            jax.ShapeDtypeStruct((PEER_NE, T), jnp.int32),
            jax.ShapeDtypeStruct((T, PEER_NE), F32),
        ],
        scratch_shapes=[
            pltpu.VMEM((2 * PEER_HEADS, PEER_NKEYS, ts), F32),
            pltpu.VMEM((2 * PEER_HEADS, PEER_TOPK, ts), F32),
            pltpu.VMEM((2 * PEER_HEADS, PEER_TOPK, ts), jnp.int32),
            pltpu.VMEM((PEER_NE, ts), F32),
        ],
        compiler_params=pltpu.CompilerParams(dimension_semantics=("arbitrary", "arbitrary"),
                                             vmem_limit_bytes=VMEM_LIMIT_BYTES),
        name="route",
    )(x, mixed_m, mixed_d, woa, wob, mod4, mod4, mod4, n2g, wq, keys)


_SQRT_HALF = float(np.sqrt(0.5))


def _peer_kernel(idx_ref, h_ref, gw_ref, x1_ref, g2_ref, tab_ref, o_ref, buf, sem, *, tt):
    ne = PEER_NE

    def issue(t, slot):
        for e in range(ne):
            row = idx_ref[e, t]
            pltpu.make_async_copy(tab_ref.at[pl.ds(row, 1), :], buf.at[slot, pl.ds(e, 1), :], sem.at[slot]).start()

    def wait(slot):
        pltpu.make_async_copy(tab_ref.at[pl.ds(0, ne), :], buf.at[slot], sem.at[slot]).wait()

    for s in range(PEER_NSLOT - 1):
        issue(s, s)

    def body(gi, carry):
        base = pl.multiple_of(gi * 8, 8)
        h8 = h_ref[pl.ds(base, 8), :].astype(BF16)
        g8 = gw_ref[pl.ds(base, 8), :]
        ys = []
        for j in range(8):
            t = base + j
            nxt = t + PEER_NSLOT - 1

            @pl.when(nxt < tt)
            def _():
                issue(nxt, (j + PEER_NSLOT - 1) % PEER_NSLOT)

            slot = j % PEER_NSLOT
            wait(slot)
            w = pltpu.bitcast(buf[slot], BF16)
            a2 = lax.dot_general(h8, w, _NT, preferred_element_type=F32)[j:j + 1]
            a = pltpu.roll(a2, 1, 1)
            act = 0.5 * a * (1.0 + lax.erf(a * _SQRT_HALF))
            coef = (act * g8[j:j + 1]).astype(BF16)
            ys.append(jnp.dot(coef, w, preferred_element_type=F32))
        y = jnp.concatenate(ys, axis=0)
        o_ref[pl.ds(base, 8), :] = x1_ref[pl.ds(base, 8), :] + g2_ref[0, 0] * y
        return carry

    lax.fori_loop(0, tt // 8, body, 0)


def _peer(eidx_t, h2, gw2, x1, mod4, table, B, S):
    T, D = h2.shape
    tt = min(PEER_TT, S)
    nst = S // tt
    return pl.pallas_call(
        functools.partial(_peer_kernel, tt=tt),
        grid=(B, nst),
        in_specs=[
            pl.BlockSpec((PEER_NE, tt), lambda b, s: (0, b * nst + s), memory_space=pltpu.SMEM),
            pl.BlockSpec((tt, D), lambda b, s: (b * nst + s, 0)),
            pl.BlockSpec((tt, 2 * PEER_NE), lambda b, s: (b * nst + s, 0)),
            pl.BlockSpec((tt, D), lambda b, s: (b * nst + s, 0)),
            pl.BlockSpec((1, 1, 1, D), lambda b, s: (b, 5, 0, 0)),
            pl.BlockSpec(memory_space=pl.ANY),
        ],
        out_specs=pl.BlockSpec((tt, D), lambda b, s: (b * nst + s, 0)),
        out_shape=jax.ShapeDtypeStruct((T, D), F32),
        scratch_shapes=[pltpu.VMEM((PEER_NSLOT, PEER_NE, D), jnp.uint32),
                        pltpu.SemaphoreType.DMA((PEER_NSLOT,))],
        compiler_params=pltpu.CompilerParams(dimension_semantics=("arbitrary", "arbitrary"),
                                             vmem_limit_bytes=VMEM_LIMIT_BYTES),
        name="peer_gather",
    )(eidx_t, h2, gw2, x1, mod4, table)


def _rope_tables(S):
    half = MLA_ROPE // 2
    inv = 1.0 / (ROPE_THETA ** (jnp.arange(half, dtype=F32) / half))
    ang = jnp.arange(S, dtype=F32)[:, None] * inv[None, :]
    cos, sin = jnp.cos(ang), jnp.sin(ang)
    cosf = jnp.tile(cos, (1, 4))
    sinf = jnp.concatenate([-sin, sin, -sin, sin], axis=1)
    return cosf, sinf


def _pad_lanes(v, fill):
    return jnp.concatenate([v, jnp.full((LANES - v.shape[0],), fill, v.dtype)]).reshape(1, LANES)


def _pack_table(u, v):
    ub = lax.bitcast_convert_type(u.astype(BF16), jnp.uint16).astype(jnp.uint32)
    vb = lax.bitcast_convert_type(v.astype(BF16), jnp.uint16).astype(jnp.uint32)
    return ub | (vb << 16)


def kernel(x, c, ada_w, ada_b, norm1_g, w_in, mla_q_lat_g, mla_w_q_up, mla_kv_lat_g, mla_w_kv_up, mla_q_g, mla_k_g, diff_q_g, diff_k_g, diff_lq1, diff_lk1, diff_lq2, diff_lk2, diff_subln_g, w_out, norm2_g, peer_w_q, peer_sub_keys, peer_u, peer_v):
    B, S, D = x.shape
    depth = ada_w.shape[0]
    cosf, sinf = _rope_tables(S)
    o0 = MLA_Q_LORA
    o1 = o0 + MLA_KV_LORA
    o2 = o1 + MLA_ROPE
    nq = DIFF_HEADS * 2 * DIFF_DQK
    o3 = o2 + nq
    o4 = o3 + nq
    for l in range(depth):
        lambda_init = 0.8 - 0.6 * math.exp(-0.3 * l)
        mod, lam = _ada(c, ada_w[l], ada_b[l], diff_lq1[l], diff_lk1[l], diff_lq2[l], diff_lk2[l], lambda_init)
        mod4 = mod.reshape(B, 6, 1, D)

        wi = w_in[l]
        win_p = jnp.concatenate([wi[:, :o1], wi[:, o2:], wi[:, o1:o2], jnp.zeros((D, LANES - MLA_ROPE), F32)],
                                axis=1).astype(BF16)
        wq3 = mla_w_q_up[l].reshape(MLA_Q_LORA, MLA_HEADS, MLA_QK)
        wq_p = jnp.concatenate([wq3, jnp.zeros((MLA_Q_LORA, MLA_HEADS, 256 - MLA_QK), F32)], axis=2)
        wq_p = wq_p.reshape(MLA_Q_LORA, MLA_HEADS * 256).astype(BF16)
        qm, km, vm, qd, kd, vd = _proj(
            x, mod4, norm1_g[l].reshape(1, D), win_p, mla_q_lat_g[l].reshape(1, -1), wq_p,
            mla_kv_lat_g[l].reshape(1, -1), mla_w_kv_up[l].astype(BF16),
            mla_q_g[l, :MLA_NOPE].reshape(1, -1), _pad_lanes(mla_q_g[l, MLA_NOPE:], 1.0),
            mla_k_g[l, :MLA_NOPE].reshape(1, -1), _pad_lanes(mla_k_g[l, MLA_NOPE:], 1.0),
            jnp.tile(diff_q_g[l], 2).reshape(1, -1), jnp.tile(diff_k_g[l], 2).reshape(1, -1), cosf, sinf)

        mixed_m = _mla_attention(qm, km, vm)
        mixed_d = _diff_attention(lam, diff_subln_g[l].reshape(1, -1), qd, kd, vd, 1.0 - lambda_init)

        wo = w_out[l].astype(BF16)
        nm = MLA_HEADS * MLA_V
        keys = peer_sub_keys[l].reshape(2 * PEER_HEADS, PEER_NKEYS, PEER_DK_HALF).astype(BF16)
        x1, h2, eidx_t, gw = _route(x, mixed_m, mixed_d, wo[:nm], wo[nm:], mod4, norm2_g[l].reshape(1, D),
                                    peer_w_q[l].astype(BF16), keys)

        table = _pack_table(peer_u[l], peer_v[l])
        gw2 = jnp.stack([jnp.zeros_like(gw), gw], axis=-1).reshape(B * S, 2 * PEER_NE)
        x = _peer(eidx_t, h2.reshape(B * S, D), gw2, x1.reshape(B * S, D), mod4, table, B, S).reshape(B, S, D)
    return x
```

```python
import functools
import math

import jax
import jax.numpy as jnp
import numpy as np
from jax import lax
from jax.experimental import pallas as pl
from jax.experimental.pallas import tpu as pltpu

F32 = jnp.float32
BF16 = jnp.bfloat16

MLA_HEADS = 4
MLA_NOPE = 128
MLA_ROPE = 64
MLA_QK = MLA_NOPE + MLA_ROPE
MLA_V = 128
MLA_Q_LORA = 384
MLA_KV_LORA = 256
DIFF_HEADS = 4
DIFF_DQK = 64
DIFF_DV = 128
ROPE_THETA = 10000.0
RMS_EPS = 1e-6
PEER_HEADS = 8
PEER_NKEYS = 128
PEER_DK_HALF = 128
PEER_TOPK = 16
NEG_INF = float("-inf")

LANES = 128
VMEM_LIMIT_BYTES = 56 * 1024 * 1024

PROJ_TS = 512
ATT_TQ = 256
ATT_TK = 512
ROUTE_TS = 256
PEER_TT = 128
PEER_NSLOT = 4
PEER_NE = PEER_HEADS * PEER_TOPK
PEER_ROW_TILE = 8


def _rms(x, n):
    ss = jnp.sum(x * x, axis=-1, keepdims=True)
    return x * lax.rsqrt(ss * (1.0 / n) + RMS_EPS)


def _swap_halves64(x):
    lane = lax.broadcasted_iota(jnp.int32, x.shape, x.ndim - 1)
    first = (lane % 64) < 32
    return jnp.where(first, pltpu.roll(x, 96, x.ndim - 1), pltpu.roll(x, 32, x.ndim - 1))


def _rope(x, cosf, sinf):
    return x * cosf + _swap_halves64(x) * sinf


def _ada_kernel(c_ref, w_ref, b_ref, lq1_ref, lk1_ref, lq2_ref, lk2_ref, mod_ref, lam_ref, *, lambda_init):
    c = c_ref[...]
    s = c * jax.nn.sigmoid(c)
    mod_ref[...] = jnp.dot(s, w_ref[...], preferred_element_type=F32) + b_ref[...]
    d1 = jnp.sum(lq1_ref[...] * lk1_ref[...], axis=-1, keepdims=True)
    d2 = jnp.sum(lq2_ref[...] * lk2_ref[...], axis=-1, keepdims=True)
    lam = jnp.exp(d1) - jnp.exp(d2) + lambda_init
    lam_ref[...] = jnp.broadcast_to(lam, lam_ref.shape)


def _ada(c, ada_w, ada_b, lq1, lk1, lq2, lk2, lambda_init):
    B, D = c.shape
    N = ada_w.shape[1]
    bn = 1024
    small = pl.BlockSpec((1, DIFF_DQK), lambda j: (0, 0))
    return pl.pallas_call(
        functools.partial(_ada_kernel, lambda_init=lambda_init),
        grid=(N // bn,),
        in_specs=[
            pl.BlockSpec((B, D), lambda j: (0, 0)),
            pl.BlockSpec((D, bn), lambda j: (0, j)),
            pl.BlockSpec((1, bn), lambda j: (0, j)),
            small, small, small, small,
        ],
        out_specs=[pl.BlockSpec((B, bn), lambda j: (0, j)), pl.BlockSpec((1, LANES), lambda j: (0, 0))],
        out_shape=[jax.ShapeDtypeStruct((B, N), F32), jax.ShapeDtypeStruct((1, LANES), F32)],
        compiler_params=pltpu.CompilerParams(dimension_semantics=("arbitrary",)),
        name="ada",
    )(c, ada_w, ada_b.reshape(1, N), lq1.reshape(1, -1), lk1.reshape(1, -1), lq2.reshape(1, -1), lk2.reshape(1, -1))


C_QLAT = 0
C_KVLAT = C_QLAT + MLA_Q_LORA
C_DQ = C_KVLAT + MLA_KV_LORA
C_DK = C_DQ + DIFF_HEADS * 2 * DIFF_DQK
C_DV = C_DK + DIFF_HEADS * 2 * DIFF_DQK
C_KPE = C_DV + DIFF_HEADS * DIFF_DV
C_END = C_KPE + LANES


def _proj_kernel(x_ref, sc_ref, sh_ref, n1g_ref, win_ref, qlg_ref, wq_ref, kvlg_ref, wkv_ref,
                 qgn_ref, qgr_ref, kgn_ref, kgr_ref, dqg_ref, dkg_ref, cos_ref, sin_ref,
                 qm_ref, km_ref, vm_ref, qd_ref, kd_ref, vd_ref):
    x = x_ref[0]
    D = x.shape[-1]
    h = _rms(x, D) * n1g_ref[...] * (1.0 + sc_ref[0, 0]) + sh_ref[0, 0]
    proj = jnp.dot(h.astype(BF16), win_ref[...], preferred_element_type=F32)
    cosf = cos_ref[...]
    sinf = sin_ref[...]

    q_lat = proj[:, C_QLAT:C_QLAT + MLA_Q_LORA]
    q = jnp.dot((_rms(q_lat, MLA_Q_LORA) * qlg_ref[...]).astype(BF16), wq_ref[...], preferred_element_type=F32)
    kv_lat = proj[:, C_KVLAT:C_KVLAT + MLA_KV_LORA]
    kv = jnp.dot((_rms(kv_lat, MLA_KV_LORA) * kvlg_ref[...]).astype(BF16), wkv_ref[...], preferred_element_type=F32)
    kpe = _rope(_rms(proj[:, C_KPE:C_KPE + LANES], MLA_ROPE) * kgr_ref[...], cosf, sinf).astype(BF16)
    q_scale = MLA_QK ** -0.5
    for hd in range(MLA_HEADS):
        qn = _rms(q[:, hd * 256:hd * 256 + 128], MLA_NOPE) * qgn_ref[...]
        qr = _rope(_rms(q[:, hd * 256 + 128:hd * 256 + 256], MLA_ROPE) * qgr_ref[...], cosf, sinf)
        qm_ref[0, hd, :, 0:128] = (qn * q_scale).astype(BF16)
        qm_ref[0, hd, :, 128:256] = (qr * q_scale).astype(BF16)
        kn = _rms(kv[:, hd * 256:hd * 256 + 128], MLA_NOPE) * kgn_ref[...]
        km_ref[0, hd, :, 0:128] = kn.astype(BF16)
        km_ref[0, hd, :, 128:256] = kpe
        vm_ref[0, hd] = kv[:, hd * 256 + 128:hd * 256 + 256].astype(BF16)

    lane = lax.broadcasted_iota(jnp.int32, (x.shape[0], LANES), 1)
    first = lane < DIFF_DQK

    def seg_norm(t):
        t2 = t * t
        s1 = jnp.sum(jnp.where(first, t2, 0.0), axis=-1, keepdims=True)
        s2 = jnp.sum(jnp.where(first, 0.0, t2), axis=-1, keepdims=True)
        r = jnp.where(first, lax.rsqrt(s1 * (1.0 / DIFF_DQK) + RMS_EPS), lax.rsqrt(s2 * (1.0 / DIFF_DQK) + RMS_EPS))
        return t * r

    d_scale = DIFF_DQK ** -0.5
    for hd in range(DIFF_HEADS):
        tq = _rope(seg_norm(proj[:, C_DQ + hd * 128:C_DQ + (hd + 1) * 128]) * dqg_ref[...], cosf, sinf) * d_scale
        qd_ref[0, hd, 0] = jnp.where(first, tq, 0.0).astype(BF16)
        qd_ref[0, hd, 1] = jnp.where(first, 0.0, tq).astype(BF16)
        tk = _rope(seg_norm(proj[:, C_DK + hd * 128:C_DK + (hd + 1) * 128]) * dkg_ref[...], cosf, sinf)
        kd_ref[0, hd] = tk.astype(BF16)
        vd_ref[0, hd] = proj[:, C_DV + hd * 128:C_DV + (hd + 1) * 128].astype(BF16)


def _proj(x, mod4, n1g, win_p, qlg, wq_p, kvlg, wkv, qgn, qgr, kgn, kgr, dqg, dkg, cosf, sinf):
    B, S, D = x.shape
    ts = min(PROJ_TS, S)
    row = lambda n: pl.BlockSpec((1, n), lambda b, s: (0, 0))
    full = lambda a: pl.BlockSpec(a.shape, lambda b, s: (0, 0))
    head_out = lambda w: pl.BlockSpec((1, MLA_HEADS, ts, w), lambda b, s: (b, 0, s, 0))
    return pl.pallas_call(
        _proj_kernel,
        grid=(B, S // ts),
        in_specs=[
            pl.BlockSpec((1, ts, D), lambda b, s: (b, s, 0)),
            pl.BlockSpec((1, 1, 1, D), lambda b, s: (b, 1, 0, 0)),
            pl.BlockSpec((1, 1, 1, D), lambda b, s: (b, 0, 0, 0)),
            row(D), full(win_p), row(MLA_Q_LORA), full(wq_p), row(MLA_KV_LORA), full(wkv),
            row(LANES), row(LANES), row(LANES), row(LANES), row(LANES), row(LANES),
            pl.BlockSpec((ts, LANES), lambda b, s: (s, 0)),
            pl.BlockSpec((ts, LANES), lambda b, s: (s, 0)),
        ],
        out_specs=[
            head_out(256), head_out(256), head_out(128),
            pl.BlockSpec((1, DIFF_HEADS, 2, ts, LANES), lambda b, s: (b, 0, 0, s, 0)),
            head_out(128), head_out(128),
        ],
        out_shape=[
            jax.ShapeDtypeStruct((B, MLA_HEADS, S, 256), BF16),
            jax.ShapeDtypeStruct((B, MLA_HEADS, S, 256), BF16),
            jax.ShapeDtypeStruct((B, MLA_HEADS, S, MLA_V), BF16),
            jax.ShapeDtypeStruct((B, DIFF_HEADS, 2, S, LANES), BF16),
            jax.ShapeDtypeStruct((B, DIFF_HEADS, S, LANES), BF16),
            jax.ShapeDtypeStruct((B, DIFF_HEADS, S, DIFF_DV), BF16),
        ],
        compiler_params=pltpu.CompilerParams(dimension_semantics=("arbitrary", "arbitrary"),
                                             vmem_limit_bytes=VMEM_LIMIT_BYTES),
        name="proj",
    )(x, mod4, mod4, n1g, win_p, qlg, wq_p, kvlg, wkv, qgn, qgr, kgn, kgr, dqg, dkg, cosf, sinf)


def _online_step(s, v, m, l, acc):
    m_new = jnp.maximum(m, jnp.max(s, axis=-1, keepdims=True))
    p = jnp.exp(s - m_new)
    alpha = jnp.exp(m - m_new)
    l = alpha * l + jnp.sum(p, axis=-1, keepdims=True)
    acc = alpha * acc + jnp.dot(p.astype(BF16), v, preferred_element_type=F32)
    return m_new, l, acc


def _causal_mask(qi, j, tq, tk):
    qpos = qi * tq + lax.broadcasted_iota(jnp.int32, (tq, tk), 0)
    kpos = j * tk + lax.broadcasted_iota(jnp.int32, (tq, tk), 1)
    return kpos <= qpos


_NT = (((1,), (1,)), ((), ()))


def _mla_kernel(q_ref, k_ref, v_ref, o_ref, *, tq, tk):
    qi = pl.program_id(2)
    q = q_ref[0, 0]
    nkv = ((qi + 1) * tq + tk - 1) // tk

    def body(j, carry):
        m, l, acc = carry
        off = pl.multiple_of(j * tk, tk)
        k = k_ref[0, 0, pl.ds(off, tk), :]
        v = v_ref[0, 0, pl.ds(off, tk), :]
        s = lax.dot_general(q, k, _NT, preferred_element_type=F32)
        s = jnp.where(_causal_mask(qi, j, tq, tk), s, NEG_INF)
        return _online_step(s, v, m, l, acc)

    init = (jnp.full((tq, 1), NEG_INF, F32), jnp.zeros((tq, 1), F32), jnp.zeros((tq, MLA_V), F32))
    m, l, acc = lax.fori_loop(0, nkv, body, init)
    o_ref[0] = (acc / l).astype(o_ref.dtype)


def _diff_kernel(lam_ref, g_ref, q_ref, k_ref, v_ref, o_ref, *, tq, tk, out_scale):
    qi = pl.program_id(2)
    q1 = q_ref[0, 0, 0]
    q2 = q_ref[0, 0, 1]
    nkv = ((qi + 1) * tq + tk - 1) // tk

    def body(j, carry):
        m1, l1, a1, m2, l2, a2 = carry
        off = pl.multiple_of(j * tk, tk)
        k = k_ref[0, 0, pl.ds(off, tk), :]
        v = v_ref[0, 0, pl.ds(off, tk), :]
        mask = _causal_mask(qi, j, tq, tk)
        s1 = jnp.where(mask, lax.dot_general(q1, k, _NT, preferred_element_type=F32), NEG_INF)
        m1, l1, a1 = _online_step(s1, v, m1, l1, a1)
        s2 = jnp.where(mask, lax.dot_general(q2, k, _NT, preferred_element_type=F32), NEG_INF)
        m2, l2, a2 = _online_step(s2, v, m2, l2, a2)
        return m1, l1, a1, m2, l2, a2

    one = (jnp.full((tq, 1), NEG_INF, F32), jnp.zeros((tq, 1), F32), jnp.zeros((tq, DIFF_DV), F32))
    m1, l1, a1, m2, l2, a2 = lax.fori_loop(0, nkv, body, one + one)
    o = a1 / l1 - lam_ref[...] * (a2 / l2)
    o = _rms(o, DIFF_DV) * g_ref[...] * out_scale
    o_ref[0] = o.astype(o_ref.dtype)


def _mla_attention(qm, km, vm):
    B, H, S, _ = qm.shape
    tq, tk = min(ATT_TQ, S), min(ATT_TK, S)
    return pl.pallas_call(
        functools.partial(_mla_kernel, tq=tq, tk=tk),
        grid=(B, H, S // tq),
        in_specs=[
            pl.BlockSpec((1, 1, tq, 256), lambda b, h, i: (b, h, i, 0)),
            pl.BlockSpec((1, 1, S, 256), lambda b, h, i: (b, h, 0, 0)),
            pl.BlockSpec((1, 1, S, MLA_V), lambda b, h, i: (b, h, 0, 0)),
        ],
        out_specs=pl.BlockSpec((1, tq, MLA_V), lambda b, h, i: (b, i, h)),
        out_shape=jax.ShapeDtypeStruct((B, S, H * MLA_V), BF16),
        compiler_params=pltpu.CompilerParams(dimension_semantics=("arbitrary",) * 3,
                                             vmem_limit_bytes=VMEM_LIMIT_BYTES),
        name="mla_attn",
    )(qm, km, vm)


def _diff_attention(lam, subln_g, qd, kd, vd, out_scale):
    B, H, _, S, _ = qd.shape
    tq, tk = min(ATT_TQ, S), min(ATT_TK, S)
    return pl.pallas_call(
        functools.partial(_diff_kernel, tq=tq, tk=tk, out_scale=out_scale),
        grid=(B, H, S // tq),
        in_specs=[
            pl.BlockSpec((1, LANES), lambda b, h, i: (0, 0)),
            pl.BlockSpec((1, DIFF_DV), lambda b, h, i: (0, 0)),
            pl.BlockSpec((1, 1, 2, tq, LANES), lambda b, h, i: (b, h, 0, i, 0)),
            pl.BlockSpec((1, 1, S, LANES), lambda b, h, i: (b, h, 0, 0)),
            pl.BlockSpec((1, 1, S, DIFF_DV), lambda b, h, i: (b, h, 0, 0)),
        ],
        out_specs=pl.BlockSpec((1, tq, DIFF_DV), lambda b, h, i: (b, i, h)),
        out_shape=jax.ShapeDtypeStruct((B, S, H * DIFF_DV), BF16),
        compiler_params=pltpu.CompilerParams(dimension_semantics=("arbitrary",) * 3,
                                             vmem_limit_bytes=VMEM_LIMIT_BYTES),
        name="diff_attn",
    )(lam, subln_g, qd, kd, vd)


def _topk_rows(vals, k, extra=None):
    n = vals.shape[0]
    row = lax.broadcasted_iota(jnp.int32, vals.shape, 0)
    out_v, out_i, out_e = [], [], []
    for _ in range(k):
        m = jnp.max(vals, axis=0, keepdims=True)
        sel = jnp.min(jnp.where(vals == m, row, n), axis=0, keepdims=True)
        hit = row == sel
        out_v.append(m)
        out_i.append(sel)
        if extra is not None:
            out_e.append(jnp.max(jnp.where(hit, extra, -1), axis=0, keepdims=True))
        vals = jnp.where(hit, NEG_INF, vals)
    cat = lambda xs: jnp.concatenate(xs, axis=0)
    return cat(out_v), cat(out_i), (cat(out_e) if extra is not None else None)


def _route_kernel(x_ref, mm_ref, md_ref, woa_ref, wob_ref, g1_ref, sc_ref, sh_ref, n2g_ref, wq_ref, keys_ref,
                  x1_ref, h2_ref, eidx_ref, gt_ref, st_scr, sv_scr, si_scr, gt_scr, et_scr):
    x = x_ref[0]
    D = x.shape[-1]
    o = jnp.dot(mm_ref[0], woa_ref[...], preferred_element_type=F32)
    o = o + jnp.dot(md_ref[0], wob_ref[...], preferred_element_type=F32)
    x1 = x + g1_ref[0, 0] * o
    x1_ref[0] = x1
    h2 = _rms(x1, D) * n2g_ref[...] * (1.0 + sc_ref[0, 0]) + sh_ref[0, 0]
    h2_ref[0] = h2
    q = jnp.dot(h2.astype(BF16), wq_ref[...], preferred_element_type=F32).astype(BF16)
    ngroups = 2 * PEER_HEADS
    for g in range(ngroups):
        st_scr[g] = lax.dot_general(keys_ref[g], q[:, g * PEER_DK_HALF:(g + 1) * PEER_DK_HALF], _NT,
                                    preferred_element_type=F32)

    def sub_topk(g, carry):
        v, i, _ = _topk_rows(st_scr[g], PEER_TOPK)
        sv_scr[g] = v
        si_scr[g] = i
        return carry

    lax.fori_loop(0, ngroups, sub_topk, 0)

    def head_topk(hd, carry):
        v0, v1 = sv_scr[2 * hd], sv_scr[2 * hd + 1]
        i0, i1 = si_scr[2 * hd], si_scr[2 * hd + 1]
        cand = jnp.concatenate([v0[a:a + 1] + v1 for a in range(PEER_TOPK)], axis=0)
        cidx = jnp.concatenate([i0[a:a + 1] * PEER_NKEYS + i1 for a in range(PEER_TOPK)], axis=0)
        fv, _, e = _topk_rows(cand, PEER_TOPK, extra=cidx)
        p = jnp.exp(fv - fv[0:1])
        gate = p / jnp.sum(p, axis=0, keepdims=True)
        r0 = pl.multiple_of(hd * PEER_TOPK, PEER_TOPK)
        et_scr[pl.ds(r0, PEER_TOPK), :] = e * PEER_ROW_TILE
        gt_scr[pl.ds(r0, PEER_TOPK), :] = gate
        return carry

    lax.fori_loop(0, PEER_HEADS, head_topk, 0)
    gt_ref[...] = gt_scr[...]
    eidx_ref[...] = et_scr[...].T


def _route(x, mixed_m, mixed_d, woa, wob, mod4, n2g, wq, keys):
    B, S, D = x.shape
    T = B * S
    ts = min(ROUTE_TS, S)
    nst = S // ts
    full = lambda a: pl.BlockSpec(a.shape, lambda b, s: (0,) * a.ndim)
    modspec = lambda k: pl.BlockSpec((1, 1, 1, D), lambda b, s: (b, k, 0, 0))
    tok = lambda w: pl.BlockSpec((1, ts, w), lambda b, s: (b, s, 0))
    return pl.pallas_call(
        _route_kernel,
        grid=(B, nst),
        in_specs=[
            tok(D), tok(mixed_m.shape[-1]), tok(mixed_d.shape[-1]), full(woa), full(wob),
            modspec(2), modspec(4), modspec(3),
            pl.BlockSpec((1, D), lambda b, s: (0, 0)), full(wq), full(keys),
        ],
        out_specs=[
            tok(D), tok(D),
            pl.BlockSpec((ts, PEER_NE), lambda b, s: (b * nst + s, 0)),
            pl.BlockSpec((PEER_NE, ts), lambda b, s: (0, b * nst + s)),
        ],
        out_shape=[
            jax.ShapeDtypeStruct((B, S, D), F32),
            jax.ShapeDtypeStruct((B, S, D), F32),
            jax.ShapeDtypeStruct((T, PEER_NE), jnp.int32),
            jax.ShapeDtypeStruct((PEER_NE, T), F32),
        ],
        scratch_shapes=[
            pltpu.VMEM((2 * PEER_HEADS, PEER_NKEYS, ts), F32),
            pltpu.VMEM((2 * PEER_HEADS, PEER_TOPK, ts), F32),
            pltpu.VMEM((2 * PEER_HEADS, PEER_TOPK, ts), jnp.int32),
            pltpu.VMEM((PEER_NE, ts), F32),
            pltpu.VMEM((PEER_NE, ts), jnp.int32),
        ],
        compiler_params=pltpu.CompilerParams(dimension_semantics=("arbitrary", "arbitrary"),
                                             vmem_limit_bytes=VMEM_LIMIT_BYTES),
        name="route",
    )(x, mixed_m, mixed_d, woa, wob, mod4, mod4, mod4, n2g, wq, keys)


_SQRT_HALF = float(np.sqrt(0.5))


def _gelu(a):
    return 0.5 * a * (1.0 + lax.erf(a * _SQRT_HALF))


def _peer_kernel(idx_ref, idxn_ref, h_ref, gt_ref, x1_ref, g2_ref, tab_ref, o_ref, buf, sem, *, tt):
    ne = PEER_NE
    rt = PEER_ROW_TILE
    look = PEER_NSLOT - 1
    nchunk = h_ref.shape[-1] // LANES
    ngrp = ne // 8
    step = pl.program_id(0) * pl.num_programs(1) + pl.program_id(1)
    nsteps = pl.num_programs(0) * pl.num_programs(1)
    nparts = 2 * ngrp
    per = ne // nparts

    def issue_part(iref, t, slot, part):
        for e in range(part * per, (part + 1) * per):
            row = pl.multiple_of(iref[t, e], rt)
            g, r = divmod(e, 8)
            pltpu.make_async_copy(tab_ref.at[pl.ds(row, rt), :], buf.at[slot, pl.ds(g * nchunk, nchunk), r, :],
                                  sem.at[slot]).start()

    def wait(slot):
        pltpu.make_async_copy(buf.at[slot], buf.at[slot], sem.at[slot]).wait()

    @pl.when(step == 0)
    def _():
        for t in range(look):
            for part in range(nparts):
                issue_part(idx_ref, t, t, part)

    lane = lax.broadcasted_iota(jnp.int32, (ne, tt), 1)

    def group(base, nxt_of):
        h8 = h_ref[pl.ds(base, 8), :]
        ys = []
        for j in range(8):
            iref, tn = nxt_of(j)
            nslot = (j + look) % PEER_NSLOT
            slot = j % PEER_NSLOT
            wait(slot)
            hrow = h8[j:j + 1]
            gate = jnp.sum(jnp.where(lane == base + j, gt_ref[...], 0.0), axis=1, keepdims=True)
            acts = []
            for g in range(ngrp):
                issue_part(iref, tn, nslot, g)
                acc = jnp.zeros((8, LANES), F32)
                for c in range(nchunk):
                    w = buf[slot, g * nchunk + c]
                    u = lax.bitcast_convert_type(w << 16, F32)
                    acc = acc + u * hrow[:, c * LANES:(c + 1) * LANES]
                a = jnp.sum(acc, axis=1, keepdims=True)
                acts.append(_gelu(a) * gate[g * 8:(g + 1) * 8])
            yacc = [jnp.zeros((8, LANES), F32) for _ in range(nchunk)]
            for g in range(ngrp):
                issue_part(iref, tn, nslot, ngrp + g)
                for c in range(nchunk):
                    w = buf[slot, g * nchunk + c]
                    v = lax.bitcast_convert_type(w & jnp.uint32(0xFFFF0000), F32)
                    yacc[c] = yacc[c] + v * acts[g]
            ys.append(jnp.concatenate([jnp.sum(yc, axis=0, keepdims=True) for yc in yacc], axis=1))
        y = jnp.concatenate(ys, axis=0)
        o_ref[pl.ds(base, 8), :] = x1_ref[pl.ds(base, 8), :] + g2_ref[0, 0] * y

    def body(gi, carry):
        base = pl.multiple_of(gi * 8, 8)
        group(base, lambda j: (idx_ref, base + j + look))
        return carry

    lax.fori_loop(0, tt // 8 - 1, body, 0)
    last = tt - 8
    group(last, lambda j: (idx_ref, last + j + look) if j + look < 8 else (idxn_ref, j + look - 8))

    @pl.when(step == nsteps - 1)
    def _():
        for t in range(look):
            wait(t)


def _peer(eidx, h2, gt, x1, mod4, table, B, S):
    T, D = h2.shape
    tt = min(PEER_TT, S)
    nst = S // tt
    nblk = B * nst
    nchunk = D // LANES
    return pl.pallas_call(
        functools.partial(_peer_kernel, tt=tt),
        grid=(B, nst),
        in_specs=[
            pl.BlockSpec((tt, PEER_NE), lambda b, s: (b * nst + s, 0), memory_space=pltpu.SMEM),
            pl.BlockSpec((tt, PEER_NE), lambda b, s: (jnp.minimum(b * nst + s + 1, nblk - 1), 0),
                         memory_space=pltpu.SMEM),
            pl.BlockSpec((tt, D), lambda b, s: (b * nst + s, 0)),
            pl.BlockSpec((PEER_NE, tt), lambda b, s: (0, b * nst + s)),
            pl.BlockSpec((tt, D), lambda b, s: (b * nst + s, 0)),
            pl.BlockSpec((1, 1, 1, D), lambda b, s: (b, 5, 0, 0)),
            pl.BlockSpec(memory_space=pl.ANY),
        ],
        out_specs=pl.BlockSpec((tt, D), lambda b, s: (b * nst + s, 0)),
        out_shape=jax.ShapeDtypeStruct((T, D), F32),
        scratch_shapes=[pltpu.VMEM((PEER_NSLOT, PEER_NE // 8 * nchunk, 8, LANES), jnp.uint32),
                        pltpu.SemaphoreType.DMA((PEER_NSLOT,))],
        compiler_params=pltpu.CompilerParams(dimension_semantics=("arbitrary", "arbitrary"),
                                             vmem_limit_bytes=VMEM_LIMIT_BYTES),
        name="peer_gather",
    )(eidx, eidx, h2, gt, x1, mod4, table)


def _rope_tables(S):
    half = MLA_ROPE // 2
    inv = 1.0 / (ROPE_THETA ** (jnp.arange(half, dtype=F32) / half))
    ang = jnp.arange(S, dtype=F32)[:, None] * inv[None, :]
    cos, sin = jnp.cos(ang), jnp.sin(ang)
    cosf = jnp.tile(cos, (1, 4))
    sinf = jnp.concatenate([-sin, sin, -sin, sin], axis=1)
    return cosf, sinf


def _pad_lanes(v, fill):
    return jnp.concatenate([v, jnp.full((LANES - v.shape[0],), fill, v.dtype)]).reshape(1, LANES)


def _pack_table(u, v):
    ub = lax.bitcast_convert_type(u.astype(BF16), jnp.uint16).astype(jnp.uint32)
    vb = lax.bitcast_convert_type(v.astype(BF16), jnp.uint16).astype(jnp.uint32)
    return (ub | (vb << 16)).reshape(-1, LANES)


def kernel(x, c, ada_w, ada_b, norm1_g, w_in, mla_q_lat_g, mla_w_q_up, mla_kv_lat_g, mla_w_kv_up, mla_q_g, mla_k_g, diff_q_g, diff_k_g, diff_lq1, diff_lk1, diff_lq2, diff_lk2, diff_subln_g, w_out, norm2_g, peer_w_q, peer_sub_keys, peer_u, peer_v):
    B, S, D = x.shape
    depth = ada_w.shape[0]
    cosf, sinf = _rope_tables(S)
    o0 = MLA_Q_LORA
    o1 = o0 + MLA_KV_LORA
    o2 = o1 + MLA_ROPE
    nq = DIFF_HEADS * 2 * DIFF_DQK
    o3 = o2 + nq
    o4 = o3 + nq
    for l in range(depth):
        lambda_init = 0.8 - 0.6 * math.exp(-0.3 * l)
        mod, lam = _ada(c, ada_w[l], ada_b[l], diff_lq1[l], diff_lk1[l], diff_lq2[l], diff_lk2[l], lambda_init)
        mod4 = mod.reshape(B, 6, 1, D)

        wi = w_in[l]
        win_p = jnp.concatenate([wi[:, :o1], wi[:, o2:], wi[:, o1:o2], jnp.zeros((D, LANES - MLA_ROPE), F32)],
                                axis=1).astype(BF16)
        wq3 = mla_w_q_up[l].reshape(MLA_Q_LORA, MLA_HEADS, MLA_QK)
        wq_p = jnp.concatenate([wq3, jnp.zeros((MLA_Q_LORA, MLA_HEADS, 256 - MLA_QK), F32)], axis=2)
        wq_p = wq_p.reshape(MLA_Q_LORA, MLA_HEADS * 256).astype(BF16)
        qm, km, vm, qd, kd, vd = _proj(
            x, mod4, norm1_g[l].reshape(1, D), win_p, mla_q_lat_g[l].reshape(1, -1), wq_p,
            mla_kv_lat_g[l].reshape(1, -1), mla_w_kv_up[l].astype(BF16),
            mla_q_g[l, :MLA_NOPE].reshape(1, -1), _pad_lanes(mla_q_g[l, MLA_NOPE:], 1.0),
            mla_k_g[l, :MLA_NOPE].reshape(1, -1), _pad_lanes(mla_k_g[l, MLA_NOPE:], 1.0),
            jnp.tile(diff_q_g[l], 2).reshape(1, -1), jnp.tile(diff_k_g[l], 2).reshape(1, -1), cosf, sinf)

        mixed_m = _mla_attention(qm, km, vm)
        mixed_d = _diff_attention(lam, diff_subln_g[l].reshape(1, -1), qd, kd, vd, 1.0 - lambda_init)

        wo = w_out[l].astype(BF16)
        nm = MLA_HEADS * MLA_V
        keys = peer_sub_keys[l].reshape(2 * PEER_HEADS, PEER_NKEYS, PEER_DK_HALF).astype(BF16)
        x1, h2, eidx, gt = _route(x, mixed_m, mixed_d, wo[:nm], wo[nm:], mod4, norm2_g[l].reshape(1, D),
                                  peer_w_q[l].astype(BF16), keys)

        table = _pack_table(peer_u[l], peer_v[l])
        x = _peer(eidx, h2.reshape(B * S, D), gt, x1.reshape(B * S, D), mod4, table, B, S).reshape(B, S, D)
    return x
```

```python
import functools
import math

import jax
import jax.numpy as jnp
import numpy as np
from jax import lax
from jax.experimental import pallas as pl
from jax.experimental.pallas import tpu as pltpu

F32 = jnp.float32
BF16 = jnp.bfloat16

MLA_HEADS = 4
MLA_NOPE = 128
MLA_ROPE = 64
MLA_QK = MLA_NOPE + MLA_ROPE
MLA_V = 128
MLA_Q_LORA = 384
MLA_KV_LORA = 256
DIFF_HEADS = 4
DIFF_DQK = 64
DIFF_DV = 128
ROPE_THETA = 10000.0
RMS_EPS = 1e-6
PEER_HEADS = 8
PEER_NKEYS = 128
PEER_DK_HALF = 128
PEER_TOPK = 16
NEG_INF = float("-inf")

LANES = 128
VMEM_LIMIT_BYTES = 56 * 1024 * 1024

PROJ_TS = 512
ATT_T = 256
ROUTE_TS = 256
PEER_TT = 128
PEER_NSLOT = 8
PEER_NE = PEER_HEADS * PEER_TOPK
PEER_ROW_TILE = 8


def _rms(x, n):
    ss = jnp.sum(x * x, axis=-1, keepdims=True)
    return x * lax.rsqrt(ss * (1.0 / n) + RMS_EPS)


def _swap_halves64(x):
    lane = lax.broadcasted_iota(jnp.int32, x.shape, x.ndim - 1)
    first = (lane % 64) < 32
    return jnp.where(first, pltpu.roll(x, 96, x.ndim - 1), pltpu.roll(x, 32, x.ndim - 1))


def _rope(x, cosf, sinf):
    return x * cosf + _swap_halves64(x) * sinf


def _ada_kernel(c_ref, w_ref, b_ref, lq1_ref, lk1_ref, lq2_ref, lk2_ref, mod_ref, lam_ref, *, lambda_init):
    c = c_ref[...]
    s = c * jax.nn.sigmoid(c)
    mod_ref[...] = jnp.dot(s, w_ref[...], preferred_element_type=F32) + b_ref[...]
    d1 = jnp.sum(lq1_ref[...] * lk1_ref[...], axis=-1, keepdims=True)
    d2 = jnp.sum(lq2_ref[...] * lk2_ref[...], axis=-1, keepdims=True)
    lam = jnp.exp(d1) - jnp.exp(d2) + lambda_init
    lam_ref[...] = jnp.broadcast_to(lam, lam_ref.shape)


def _ada(c, ada_w, ada_b, lq1, lk1, lq2, lk2, lambda_init):
    B, D = c.shape
    N = ada_w.shape[1]
    bn = 1024
    small = pl.BlockSpec((1, DIFF_DQK), lambda j: (0, 0))
    return pl.pallas_call(
        functools.partial(_ada_kernel, lambda_init=lambda_init),
        grid=(N // bn,),
        in_specs=[
            pl.BlockSpec((B, D), lambda j: (0, 0)),
            pl.BlockSpec((D, bn), lambda j: (0, j)),
            pl.BlockSpec((1, bn), lambda j: (0, j)),
            small, small, small, small,
        ],
        out_specs=[pl.BlockSpec((B, bn), lambda j: (0, j)), pl.BlockSpec((1, LANES), lambda j: (0, 0))],
        out_shape=[jax.ShapeDtypeStruct((B, N), F32), jax.ShapeDtypeStruct((1, LANES), F32)],
        compiler_params=pltpu.CompilerParams(dimension_semantics=("arbitrary",)),
        name="ada",
    )(c, ada_w, ada_b.reshape(1, N), lq1.reshape(1, -1), lk1.reshape(1, -1), lq2.reshape(1, -1), lk2.reshape(1, -1))


C_QLAT = 0
C_KVLAT = C_QLAT + MLA_Q_LORA
C_DQ = C_KVLAT + MLA_KV_LORA
C_DK = C_DQ + DIFF_HEADS * 2 * DIFF_DQK
C_DV = C_DK + DIFF_HEADS * 2 * DIFF_DQK
C_KPE = C_DV + DIFF_HEADS * DIFF_DV
C_END = C_KPE + LANES


def _proj_kernel(x_ref, sc_ref, sh_ref, n1g_ref, win_ref, qlg_ref, wq_ref, kvlg_ref, wkv_ref,
                 qgn_ref, qgr_ref, kgn_ref, kgr_ref, dqg_ref, dkg_ref, cos_ref, sin_ref,
                 qm_ref, km_ref, vm_ref, qd_ref, kd_ref, vd_ref):
    x = x_ref[0]
    D = x.shape[-1]
    h = _rms(x, D) * n1g_ref[...] * (1.0 + sc_ref[0, 0]) + sh_ref[0, 0]
    proj = jnp.dot(h.astype(BF16), win_ref[...], preferred_element_type=F32)
    cosf = cos_ref[...]
    sinf = sin_ref[...]

    q_lat = proj[:, C_QLAT:C_QLAT + MLA_Q_LORA]
    q = jnp.dot((_rms(q_lat, MLA_Q_LORA) * qlg_ref[...]).astype(BF16), wq_ref[...], preferred_element_type=F32)
    kv_lat = proj[:, C_KVLAT:C_KVLAT + MLA_KV_LORA]
    kv = jnp.dot((_rms(kv_lat, MLA_KV_LORA) * kvlg_ref[...]).astype(BF16), wkv_ref[...], preferred_element_type=F32)
    kpe = _rope(_rms(proj[:, C_KPE:C_KPE + LANES], MLA_ROPE) * kgr_ref[...], cosf, sinf).astype(BF16)
    q_scale = MLA_QK ** -0.5
    for hd in range(MLA_HEADS):
        qn = _rms(q[:, hd * 256:hd * 256 + 128], MLA_NOPE) * qgn_ref[...]
        qr = _rope(_rms(q[:, hd * 256 + 128:hd * 256 + 256], MLA_ROPE) * qgr_ref[...], cosf, sinf)
        qm_ref[0, hd, :, 0:128] = (qn * q_scale).astype(BF16)
        qm_ref[0, hd, :, 128:256] = (qr * q_scale).astype(BF16)
        kn = _rms(kv[:, hd * 256:hd * 256 + 128], MLA_NOPE) * kgn_ref[...]
        km_ref[0, hd, :, 0:128] = kn.astype(BF16)
        km_ref[0, hd, :, 128:256] = kpe
        vm_ref[0, hd] = kv[:, hd * 256 + 128:hd * 256 + 256].astype(BF16)

    lane = lax.broadcasted_iota(jnp.int32, (x.shape[0], LANES), 1)
    first = lane < DIFF_DQK

    def seg_norm(t):
        t2 = t * t
        s1 = jnp.sum(jnp.where(first, t2, 0.0), axis=-1, keepdims=True)
        s2 = jnp.sum(jnp.where(first, 0.0, t2), axis=-1, keepdims=True)
        r = jnp.where(first, lax.rsqrt(s1 * (1.0 / DIFF_DQK) + RMS_EPS), lax.rsqrt(s2 * (1.0 / DIFF_DQK) + RMS_EPS))
        return t * r

    d_scale = DIFF_DQK ** -0.5
    for hd in range(DIFF_HEADS):
        tq = _rope(seg_norm(proj[:, C_DQ + hd * 128:C_DQ + (hd + 1) * 128]) * dqg_ref[...], cosf, sinf) * d_scale
        qd_ref[0, hd, 0] = jnp.where(first, tq, 0.0).astype(BF16)
        qd_ref[0, hd, 1] = jnp.where(first, 0.0, tq).astype(BF16)
        tk = _rope(seg_norm(proj[:, C_DK + hd * 128:C_DK + (hd + 1) * 128]) * dkg_ref[...], cosf, sinf)
        kd_ref[0, hd] = tk.astype(BF16)
        vd_ref[0, hd] = proj[:, C_DV + hd * 128:C_DV + (hd + 1) * 128].astype(BF16)


def _proj(x, mod4, n1g, win_p, qlg, wq_p, kvlg, wkv, qgn, qgr, kgn, kgr, dqg, dkg, cosf, sinf):
    B, S, D = x.shape
    ts = min(PROJ_TS, S)
    row = lambda n: pl.BlockSpec((1, n), lambda b, s: (0, 0))
    full = lambda a: pl.BlockSpec(a.shape, lambda b, s: (0, 0))
    head_out = lambda w: pl.BlockSpec((1, MLA_HEADS, ts, w), lambda b, s: (b, 0, s, 0))
    return pl.pallas_call(
        _proj_kernel,
        grid=(B, S // ts),
        in_specs=[
            pl.BlockSpec((1, ts, D), lambda b, s: (b, s, 0)),
            pl.BlockSpec((1, 1, 1, D), lambda b, s: (b, 1, 0, 0)),
            pl.BlockSpec((1, 1, 1, D), lambda b, s: (b, 0, 0, 0)),
            row(D), full(win_p), row(MLA_Q_LORA), full(wq_p), row(MLA_KV_LORA), full(wkv),
            row(LANES), row(LANES), row(LANES), row(LANES), row(LANES), row(LANES),
            pl.BlockSpec((ts, LANES), lambda b, s: (s, 0)),
            pl.BlockSpec((ts, LANES), lambda b, s: (s, 0)),
        ],
        out_specs=[
            head_out(256), head_out(256), head_out(128),
            pl.BlockSpec((1, DIFF_HEADS, 2, ts, LANES), lambda b, s: (b, 0, 0, s, 0)),
            head_out(128), head_out(128),
        ],
        out_shape=[
            jax.ShapeDtypeStruct((B, MLA_HEADS, S, 256), BF16),
            jax.ShapeDtypeStruct((B, MLA_HEADS, S, 256), BF16),
            jax.ShapeDtypeStruct((B, MLA_HEADS, S, MLA_V), BF16),
            jax.ShapeDtypeStruct((B, DIFF_HEADS, 2, S, LANES), BF16),
            jax.ShapeDtypeStruct((B, DIFF_HEADS, S, LANES), BF16),
            jax.ShapeDtypeStruct((B, DIFF_HEADS, S, DIFF_DV), BF16),
        ],
        compiler_params=pltpu.CompilerParams(dimension_semantics=("arbitrary", "arbitrary"),
                                             vmem_limit_bytes=VMEM_LIMIT_BYTES),
        name="proj",
    )(x, mod4, mod4, n1g, win_p, qlg, wq_p, kvlg, wkv, qgn, qgr, kgn, kgr, dqg, dkg, cosf, sinf)


def _online_step(s, v, m, l, acc):
    m_new = jnp.maximum(m, jnp.max(s, axis=-1, keepdims=True))
    p = jnp.exp(s - m_new)
    alpha = jnp.exp(m - m_new)
    l = alpha * l + jnp.sum(p, axis=-1, keepdims=True)
    acc = alpha * acc + jnp.dot(p.astype(BF16), v, preferred_element_type=F32)
    return m_new, l, acc


_NT = (((1,), (1,)), ((), ()))


def _diag_mask(t):
    return lax.broadcasted_iota(jnp.int32, (t, t), 0) >= lax.broadcasted_iota(jnp.int32, (t, t), 1)


def _state(t, dv):
    return (jnp.full((t, 1), NEG_INF, F32), jnp.zeros((t, 1), F32), jnp.zeros((t, dv), F32))


def _mla_kernel(q_ref, k_ref, v_ref, o_ref, *, t):
    qi = pl.program_id(1)
    nh = q_ref.shape[1]
    qs = [q_ref[0, h] for h in range(nh)]
    mask = _diag_mask(t)

    def block(j, carry, diagonal):
        off = pl.multiple_of(j * t, t)
        out = []
        for h in range(nh):
            k = k_ref[0, h, pl.ds(off, t), :]
            v = v_ref[0, h, pl.ds(off, t), :]
            s = lax.dot_general(qs[h], k, _NT, preferred_element_type=F32)
            if diagonal:
                s = jnp.where(mask, s, NEG_INF)
            out.append(_online_step(s, v, *carry[h]))
        return tuple(out)

    carry = lax.fori_loop(0, qi, lambda j, c: block(j, c, False), tuple(_state(t, MLA_V) for _ in range(nh)))
    carry = block(qi, carry, True)
    for h in range(nh):
        _, l, acc = carry[h]
        o_ref[0, :, h * MLA_V:(h + 1) * MLA_V] = (acc / l).astype(o_ref.dtype)


def _diff_kernel(lam_ref, g_ref, q_ref, k_ref, v_ref, o_ref, *, t, out_scale):
    qi = pl.program_id(1)
    nh = q_ref.shape[1]
    qs = [(q_ref[0, h, 0], q_ref[0, h, 1]) for h in range(nh)]
    mask = _diag_mask(t)

    def block(j, carry, diagonal):
        off = pl.multiple_of(j * t, t)
        out = []
        for h in range(nh):
            k = k_ref[0, h, pl.ds(off, t), :]
            v = v_ref[0, h, pl.ds(off, t), :]
            for p in range(2):
                s = lax.dot_general(qs[h][p], k, _NT, preferred_element_type=F32)
                if diagonal:
                    s = jnp.where(mask, s, NEG_INF)
                out.append(_online_step(s, v, *carry[2 * h + p]))
        return tuple(out)

    carry = lax.fori_loop(0, qi, lambda j, c: block(j, c, False),
                          tuple(_state(t, DIFF_DV) for _ in range(2 * nh)))
    carry = block(qi, carry, True)
    for h in range(nh):
        _, l1, a1 = carry[2 * h]
        _, l2, a2 = carry[2 * h + 1]
        o = a1 / l1 - lam_ref[...] * (a2 / l2)
        o = _rms(o, DIFF_DV) * g_ref[...] * out_scale
        o_ref[0, :, h * DIFF_DV:(h + 1) * DIFF_DV] = o.astype(o_ref.dtype)


def _mla_attention(qm, km, vm):
    B, H, S, _ = qm.shape
    t = min(ATT_T, S)
    return pl.pallas_call(
        functools.partial(_mla_kernel, t=t),
        grid=(B, S // t),
        in_specs=[
            pl.BlockSpec((1, H, t, 256), lambda b, i: (b, 0, i, 0)),
            pl.BlockSpec((1, H, S, 256), lambda b, i: (b, 0, 0, 0)),
            pl.BlockSpec((1, H, S, MLA_V), lambda b, i: (b, 0, 0, 0)),
        ],
        out_specs=pl.BlockSpec((1, t, H * MLA_V), lambda b, i: (b, i, 0)),
        out_shape=jax.ShapeDtypeStruct((B, S, H * MLA_V), BF16),
        compiler_params=pltpu.CompilerParams(dimension_semantics=("arbitrary",) * 2,
                                             vmem_limit_bytes=VMEM_LIMIT_BYTES),
        name="mla_attn",
    )(qm, km, vm)


def _diff_attention(lam, subln_g, qd, kd, vd, out_scale):
    B, H, _, S, _ = qd.shape
    t = min(ATT_T, S)
    return pl.pallas_call(
        functools.partial(_diff_kernel, t=t, out_scale=out_scale),
        grid=(B, S // t),
        in_specs=[
            pl.BlockSpec((1, LANES), lambda b, i: (0, 0)),
            pl.BlockSpec((1, DIFF_DV), lambda b, i: (0, 0)),
            pl.BlockSpec((1, H, 2, t, LANES), lambda b, i: (b, 0, 0, i, 0)),
            pl.BlockSpec((1, H, S, LANES), lambda b, i: (b, 0, 0, 0)),
            pl.BlockSpec((1, H, S, DIFF_DV), lambda b, i: (b, 0, 0, 0)),
        ],
        out_specs=pl.BlockSpec((1, t, H * DIFF_DV), lambda b, i: (b, i, 0)),
        out_shape=jax.ShapeDtypeStruct((B, S, H * DIFF_DV), BF16),
        compiler_params=pltpu.CompilerParams(dimension_semantics=("arbitrary",) * 2,
                                             vmem_limit_bytes=VMEM_LIMIT_BYTES),
        name="diff_attn",
    )(lam, subln_g, qd, kd, vd)


def _topk_rows(vals, k, extra=None):
    n = vals.shape[0]
    row = lax.broadcasted_iota(jnp.int32, vals.shape, 0)
    out_v, out_i, out_e = [], [], []
    for _ in range(k):
        m = jnp.max(vals, axis=0, keepdims=True)
        sel = jnp.min(jnp.where(vals == m, row, n), axis=0, keepdims=True)
        hit = row == sel
        out_v.append(m)
        out_i.append(sel)
        if extra is not None:
            out_e.append(jnp.max(jnp.where(hit, extra, -1), axis=0, keepdims=True))
        vals = jnp.where(hit, NEG_INF, vals)
    cat = lambda xs: jnp.concatenate(xs, axis=0)
    return cat(out_v), cat(out_i), (cat(out_e) if extra is not None else None)


def _route_kernel(x_ref, mm_ref, md_ref, woa_ref, wob_ref, g1_ref, sc_ref, sh_ref, n2g_ref, wq_ref, keys_ref,
                  x1_ref, h2_ref, eidx_ref, gt_ref, st_scr, sv_scr, si_scr, gt_scr, et_scr):
    x = x_ref[0]
    D = x.shape[-1]
    o = jnp.dot(mm_ref[0], woa_ref[...], preferred_element_type=F32)
    o = o + jnp.dot(md_ref[0], wob_ref[...], preferred_element_type=F32)
    x1 = x + g1_ref[0, 0] * o
    x1_ref[0] = x1
    h2 = _rms(x1, D) * n2g_ref[...] * (1.0 + sc_ref[0, 0]) + sh_ref[0, 0]
    h2_ref[0] = h2
    q = jnp.dot(h2.astype(BF16), wq_ref[...], preferred_element_type=F32).astype(BF16)
    ngroups = 2 * PEER_HEADS
    for g in range(ngroups):
        st_scr[g] = lax.dot_general(keys_ref[g], q[:, g * PEER_DK_HALF:(g + 1) * PEER_DK_HALF], _NT,
                                    preferred_element_type=F32)

    def sub_topk(g, carry):
        v, i, _ = _topk_rows(st_scr[g], PEER_TOPK)
        sv_scr[g] = v
        si_scr[g] = i
        return carry

    lax.fori_loop(0, ngroups, sub_topk, 0)

    def head_topk(hd, carry):
        v0, v1 = sv_scr[2 * hd], sv_scr[2 * hd + 1]
        i0, i1 = si_scr[2 * hd], si_scr[2 * hd + 1]
        k = PEER_TOPK
        sub = lax.broadcasted_iota(jnp.int32, (8, v0.shape[1]), 0)
        cv, ci = [v0[0:1] + v1], [i0[0:1] * PEER_NKEYS + i1]
        for a in range(1, 8):
            cv.append(jnp.where(sub < k // (a + 1), v0[a:a + 1] + v1[0:8], NEG_INF))
            ci.append(i0[a:a + 1] * PEER_NKEYS + i1[0:8])
        cv.append(v0[8:k] + v1[0:1])
        ci.append(i0[8:k] * PEER_NKEYS + i1[0:1])
        fv, _, e = _topk_rows(jnp.concatenate(cv, axis=0), k, extra=jnp.concatenate(ci, axis=0))
        p = jnp.exp(fv - fv[0:1])
        gate = p / jnp.sum(p, axis=0, keepdims=True)
        r0 = pl.multiple_of(hd * PEER_TOPK, PEER_TOPK)
        et_scr[pl.ds(r0, PEER_TOPK), :] = e * PEER_ROW_TILE
        gt_scr[pl.ds(r0, PEER_TOPK), :] = gate
        return carry

    lax.fori_loop(0, PEER_HEADS, head_topk, 0)
    gt_ref[...] = gt_scr[...]
    eidx_ref[...] = et_scr[...].T


def _route(x, mixed_m, mixed_d, woa, wob, mod4, n2g, wq, keys):
    B, S, D = x.shape
    T = B * S
    ts = min(ROUTE_TS, S)
    nst = S // ts
    full = lambda a: pl.BlockSpec(a.shape, lambda b, s: (0,) * a.ndim)
    modspec = lambda k: pl.BlockSpec((1, 1, 1, D), lambda b, s: (b, k, 0, 0))
    tok = lambda w: pl.BlockSpec((1, ts, w), lambda b, s: (b, s, 0))
    return pl.pallas_call(
        _route_kernel,
        grid=(B, nst),
        in_specs=[
            tok(D), tok(mixed_m.shape[-1]), tok(mixed_d.shape[-1]), full(woa), full(wob),
            modspec(2), modspec(4), modspec(3),
            pl.BlockSpec((1, D), lambda b, s: (0, 0)), full(wq), full(keys),
        ],
        out_specs=[
            tok(D), tok(D),
            pl.BlockSpec((ts, PEER_NE), lambda b, s: (b * nst + s, 0)),
            pl.BlockSpec((PEER_NE, ts), lambda b, s: (0, b * nst + s)),
        ],
        out_shape=[
            jax.ShapeDtypeStruct((B, S, D), F32),
            jax.ShapeDtypeStruct((B, S, D), F32),
            jax.ShapeDtypeStruct((T, PEER_NE), jnp.int32),
            jax.ShapeDtypeStruct((PEER_NE, T), F32),
        ],
        scratch_shapes=[
            pltpu.VMEM((2 * PEER_HEADS, PEER_NKEYS, ts), F32),
            pltpu.VMEM((2 * PEER_HEADS, PEER_TOPK, ts), F32),
            pltpu.VMEM((2 * PEER_HEADS, PEER_TOPK, ts), jnp.int32),
            pltpu.VMEM((PEER_NE, ts), F32),
            pltpu.VMEM((PEER_NE, ts), jnp.int32),
        ],
        compiler_params=pltpu.CompilerParams(dimension_semantics=("arbitrary", "arbitrary"),
                                             vmem_limit_bytes=VMEM_LIMIT_BYTES),
        name="route",
    )(x, mixed_m, mixed_d, woa, wob, mod4, mod4, mod4, n2g, wq, keys)


_SQRT_HALF = float(np.sqrt(0.5))


def _gelu(a):
    return 0.5 * a * (1.0 + lax.erf(a * _SQRT_HALF))


def _peer_kernel(idx_ref, idxn_ref, h_ref, gt_ref, x1_ref, g2_ref, tab_ref, o_ref, buf, sem, *, tt):
    ne = PEER_NE
    rt = PEER_ROW_TILE
    look = PEER_NSLOT - 1
    nchunk = h_ref.shape[-1] // LANES
    ngrp = ne // 8
    step = pl.program_id(0) * pl.num_programs(1) + pl.program_id(1)
    nsteps = pl.num_programs(0) * pl.num_programs(1)
    nparts = 2 * ngrp
    per = ne // nparts

    def issue_part(iref, t, slot, part):
        for e in range(part * per, (part + 1) * per):
            row = pl.multiple_of(iref[t, e], rt)
            g, r = divmod(e, 8)
            pltpu.make_async_copy(tab_ref.at[pl.ds(row, rt), :], buf.at[slot, pl.ds(g * nchunk, nchunk), r, :],
                                  sem.at[slot]).start(priority=e % 2)

    def wait(slot):
        pltpu.make_async_copy(buf.at[slot], buf.at[slot], sem.at[slot]).wait()

    @pl.when(step == 0)
    def _():
        for t in range(look):
            for part in range(nparts):
                issue_part(idx_ref, t, t, part)

    lane = lax.broadcasted_iota(jnp.int32, (ne, tt), 1)

    def group(base, nxt_of):
        h8 = h_ref[pl.ds(base, 8), :]
        ys = []
        for j in range(8):
            iref, tn = nxt_of(j)
            nslot = (j + look) % PEER_NSLOT
            slot = j % PEER_NSLOT
            wait(slot)
            hrow = h8[j:j + 1]
            gate = jnp.sum(jnp.where(lane == base + j, gt_ref[...], 0.0), axis=1, keepdims=True)
            acts = []
            for g in range(ngrp):
                issue_part(iref, tn, nslot, g)
                acc = jnp.zeros((8, LANES), F32)
                for c in range(nchunk):
                    w = buf[slot, g * nchunk + c]
                    u = lax.bitcast_convert_type(w << 16, F32)
                    acc = acc + u * hrow[:, c * LANES:(c + 1) * LANES]
                a = jnp.sum(acc, axis=1, keepdims=True)
                acts.append(_gelu(a) * gate[g * 8:(g + 1) * 8])
            yacc = [jnp.zeros((8, LANES), F32) for _ in range(nchunk)]
            for g in range(ngrp):
                issue_part(iref, tn, nslot, ngrp + g)
                for c in range(nchunk):
                    w = buf[slot, g * nchunk + c]
                    v = lax.bitcast_convert_type(w & jnp.uint32(0xFFFF0000), F32)
                    yacc[c] = yacc[c] + v * acts[g]
            ys.append(jnp.concatenate([jnp.sum(yc, axis=0, keepdims=True) for yc in yacc], axis=1))
        y = jnp.concatenate(ys, axis=0)
        o_ref[pl.ds(base, 8), :] = x1_ref[pl.ds(base, 8), :] + g2_ref[0, 0] * y

    def body(gi, carry):
        base = pl.multiple_of(gi * 8, 8)
        group(base, lambda j: (idx_ref, base + j + look))
        return carry

    lax.fori_loop(0, tt // 8 - 1, body, 0)
    last = tt - 8
    group(last, lambda j: (idx_ref, last + j + look) if j + look < 8 else (idxn_ref, j + look - 8))

    @pl.when(step == nsteps - 1)
    def _():
        for t in range(look):
            wait(t)


def _peer(eidx, h2, gt, x1, mod4, table, B, S):
    T, D = h2.shape
    tt = min(PEER_TT, S)
    nst = S // tt
    nblk = B * nst
    nchunk = D // LANES
    return pl.pallas_call(
        functools.partial(_peer_kernel, tt=tt),
        grid=(B, nst),
        in_specs=[
            pl.BlockSpec((tt, PEER_NE), lambda b, s: (b * nst + s, 0), memory_space=pltpu.SMEM),
            pl.BlockSpec((tt, PEER_NE), lambda b, s: (jnp.minimum(b * nst + s + 1, nblk - 1), 0),
                         memory_space=pltpu.SMEM),
            pl.BlockSpec((tt, D), lambda b, s: (b * nst + s, 0)),
            pl.BlockSpec((PEER_NE, tt), lambda b, s: (0, b * nst + s)),
            pl.BlockSpec((tt, D), lambda b, s: (b * nst + s, 0)),
            pl.BlockSpec((1, 1, 1, D), lambda b, s: (b, 5, 0, 0)),
            pl.BlockSpec(memory_space=pl.ANY),
        ],
        out_specs=pl.BlockSpec((tt, D), lambda b, s: (b * nst + s, 0)),
        out_shape=jax.ShapeDtypeStruct((T, D), F32),
        scratch_shapes=[pltpu.VMEM((PEER_NSLOT, PEER_NE // 8 * nchunk, 8, LANES), jnp.uint32),
                        pltpu.SemaphoreType.DMA((PEER_NSLOT,))],
        compiler_params=pltpu.CompilerParams(dimension_semantics=("arbitrary", "arbitrary"),
                                             vmem_limit_bytes=VMEM_LIMIT_BYTES),
        name="peer_gather",
    )(eidx, eidx, h2, gt, x1, mod4, table)


def _rope_tables(S):
    half = MLA_ROPE // 2
    inv = 1.0 / (ROPE_THETA ** (jnp.arange(half, dtype=F32) / half))
    ang = jnp.arange(S, dtype=F32)[:, None] * inv[None, :]
    cos, sin = jnp.cos(ang), jnp.sin(ang)
    cosf = jnp.tile(cos, (1, 4))
    sinf = jnp.concatenate([-sin, sin, -sin, sin], axis=1)
    return cosf, sinf


def _pad_lanes(v, fill):
    return jnp.concatenate([v, jnp.full((LANES - v.shape[0],), fill, v.dtype)]).reshape(1, LANES)


def _pack_table(u, v):
    ub = lax.bitcast_convert_type(u.astype(BF16), jnp.uint16).astype(jnp.uint32)
    vb = lax.bitcast_convert_type(v.astype(BF16), jnp.uint16).astype(jnp.uint32)
    return (ub | (vb << 16)).reshape(-1, LANES)


def kernel(x, c, ada_w, ada_b, norm1_g, w_in, mla_q_lat_g, mla_w_q_up, mla_kv_lat_g, mla_w_kv_up, mla_q_g, mla_k_g, diff_q_g, diff_k_g, diff_lq1, diff_lk1, diff_lq2, diff_lk2, diff_subln_g, w_out, norm2_g, peer_w_q, peer_sub_keys, peer_u, peer_v):
    B, S, D = x.shape
    depth = ada_w.shape[0]
    cosf, sinf = _rope_tables(S)
    o0 = MLA_Q_LORA
    o1 = o0 + MLA_KV_LORA
    o2 = o1 + MLA_ROPE
    nq = DIFF_HEADS * 2 * DIFF_DQK
    o3 = o2 + nq
    o4 = o3 + nq
    for l in range(depth):
        lambda_init = 0.8 - 0.6 * math.exp(-0.3 * l)
        mod, lam = _ada(c, ada_w[l], ada_b[l], diff_lq1[l], diff_lk1[l], diff_lq2[l], diff_lk2[l], lambda_init)
        mod4 = mod.reshape(B, 6, 1, D)

        wi = w_in[l]
        win_p = jnp.concatenate([wi[:, :o1], wi[:, o2:], wi[:, o1:o2], jnp.zeros((D, LANES - MLA_ROPE), F32)],
                                axis=1).astype(BF16)
        wq3 = mla_w_q_up[l].reshape(MLA_Q_LORA, MLA_HEADS, MLA_QK)
        wq_p = jnp.concatenate([wq3, jnp.zeros((MLA_Q_LORA, MLA_HEADS, 256 - MLA_QK), F32)], axis=2)
        wq_p = wq_p.reshape(MLA_Q_LORA, MLA_HEADS * 256).astype(BF16)
        qm, km, vm, qd, kd, vd = _proj(
            x, mod4, norm1_g[l].reshape(1, D), win_p, mla_q_lat_g[l].reshape(1, -1), wq_p,
            mla_kv_lat_g[l].reshape(1, -1), mla_w_kv_up[l].astype(BF16),
            mla_q_g[l, :MLA_NOPE].reshape(1, -1), _pad_lanes(mla_q_g[l, MLA_NOPE:], 1.0),
            mla_k_g[l, :MLA_NOPE].reshape(1, -1), _pad_lanes(mla_k_g[l, MLA_NOPE:], 1.0),
            jnp.tile(diff_q_g[l], 2).reshape(1, -1), jnp.tile(diff_k_g[l], 2).reshape(1, -1), cosf, sinf)

        mixed_m = _mla_attention(qm, km, vm)
        mixed_d = _diff_attention(lam, diff_subln_g[l].reshape(1, -1), qd, kd, vd, 1.0 - lambda_init)

        wo = w_out[l].astype(BF16)
        nm = MLA_HEADS * MLA_V
        keys = peer_sub_keys[l].reshape(2 * PEER_HEADS, PEER_NKEYS, PEER_DK_HALF).astype(BF16)
        x1, h2, eidx, gt = _route(x, mixed_m, mixed_d, wo[:nm], wo[nm:], mod4, norm2_g[l].reshape(1, D),
                                  peer_w_q[l].astype(BF16), keys)

        table = _pack_table(peer_u[l], peer_v[l])
        x = _peer(eidx, h2.reshape(B * S, D), gt, x1.reshape(B * S, D), mod4, table, B, S).reshape(B, S, D)
    return x
```

```python
import functools
import math

import jax
import jax.numpy as jnp
import numpy as np
from jax import lax
from jax.experimental import pallas as pl
from jax.experimental.pallas import tpu as pltpu
from jax.experimental.pallas import tpu_sc as plsc

F32 = jnp.float32
BF16 = jnp.bfloat16

MLA_HEADS = 4
MLA_NOPE = 128
MLA_ROPE = 64
MLA_QK = MLA_NOPE + MLA_ROPE
MLA_V = 128
MLA_Q_LORA = 384
MLA_KV_LORA = 256
DIFF_HEADS = 4
DIFF_DQK = 64
DIFF_DV = 128
ROPE_THETA = 10000.0
RMS_EPS = 1e-6
PEER_HEADS = 8
PEER_NKEYS = 128
PEER_DK_HALF = 128
PEER_TOPK = 16
NEG_INF = float("-inf")

LANES = 128
VMEM_LIMIT_BYTES = 56 * 1024 * 1024

PROJ_TS = 512
ATT_T = 256
ROUTE_TS = 256
PEER_TT = 128
PEER_NSLOT = 8
PEER_NE = PEER_HEADS * PEER_TOPK
PEER_ROW_TILE = 8
SC_WORKERS = 32
SC_CHUNK = 64
SC_SHARE_NUM, SC_SHARE_DEN = 13, 32


def _rms(x, n):
    ss = jnp.sum(x * x, axis=-1, keepdims=True)
    return x * lax.rsqrt(ss * (1.0 / n) + RMS_EPS)


def _swap_halves64(x):
    lane = lax.broadcasted_iota(jnp.int32, x.shape, x.ndim - 1)
    first = (lane % 64) < 32
    return jnp.where(first, pltpu.roll(x, 96, x.ndim - 1), pltpu.roll(x, 32, x.ndim - 1))


def _rope(x, cosf, sinf):
    return x * cosf + _swap_halves64(x) * sinf


def _ada_kernel(c_ref, w_ref, b_ref, lq1_ref, lk1_ref, lq2_ref, lk2_ref, mod_ref, lam_ref, *, lambda_init):
    c = c_ref[...]
    s = c * jax.nn.sigmoid(c)
    mod_ref[...] = jnp.dot(s, w_ref[...], preferred_element_type=F32) + b_ref[...]
    d1 = jnp.sum(lq1_ref[...] * lk1_ref[...], axis=-1, keepdims=True)
    d2 = jnp.sum(lq2_ref[...] * lk2_ref[...], axis=-1, keepdims=True)
    lam = jnp.exp(d1) - jnp.exp(d2) + lambda_init
    lam_ref[...] = jnp.broadcast_to(lam, lam_ref.shape)


def _ada(c, ada_w, ada_b, lq1, lk1, lq2, lk2, lambda_init):
    B, D = c.shape
    N = ada_w.shape[1]
    bn = 1024
    small = pl.BlockSpec((1, DIFF_DQK), lambda j: (0, 0))
    return pl.pallas_call(
        functools.partial(_ada_kernel, lambda_init=lambda_init),
        grid=(N // bn,),
        in_specs=[
            pl.BlockSpec((B, D), lambda j: (0, 0)),
            pl.BlockSpec((D, bn), lambda j: (0, j)),
            pl.BlockSpec((1, bn), lambda j: (0, j)),
            small, small, small, small,
        ],
        out_specs=[pl.BlockSpec((B, bn), lambda j: (0, j)), pl.BlockSpec((1, LANES), lambda j: (0, 0))],
        out_shape=[jax.ShapeDtypeStruct((B, N), F32), jax.ShapeDtypeStruct((1, LANES), F32)],
        compiler_params=pltpu.CompilerParams(dimension_semantics=("arbitrary",)),
        name="ada",
    )(c, ada_w, ada_b.reshape(1, N), lq1.reshape(1, -1), lk1.reshape(1, -1), lq2.reshape(1, -1), lk2.reshape(1, -1))


C_QLAT = 0
C_KVLAT = C_QLAT + MLA_Q_LORA
C_DQ = C_KVLAT + MLA_KV_LORA
C_DK = C_DQ + DIFF_HEADS * 2 * DIFF_DQK
C_DV = C_DK + DIFF_HEADS * 2 * DIFF_DQK
C_KPE = C_DV + DIFF_HEADS * DIFF_DV
C_END = C_KPE + LANES


def _proj_kernel(x_ref, sc_ref, sh_ref, n1g_ref, win_ref, qlg_ref, wq_ref, kvlg_ref, wkv_ref,
                 qgn_ref, qgr_ref, kgn_ref, kgr_ref, dqg_ref, dkg_ref, cos_ref, sin_ref,
                 qm_ref, km_ref, vm_ref, qd_ref, kd_ref, vd_ref):
    x = x_ref[0]
    D = x.shape[-1]
    h = _rms(x, D) * n1g_ref[...] * (1.0 + sc_ref[0, 0]) + sh_ref[0, 0]
    proj = jnp.dot(h.astype(BF16), win_ref[...], preferred_element_type=F32)
    cosf = cos_ref[...]
    sinf = sin_ref[...]

    q_lat = proj[:, C_QLAT:C_QLAT + MLA_Q_LORA]
    q = jnp.dot((_rms(q_lat, MLA_Q_LORA) * qlg_ref[...]).astype(BF16), wq_ref[...], preferred_element_type=F32)
    kv_lat = proj[:, C_KVLAT:C_KVLAT + MLA_KV_LORA]
    kv = jnp.dot((_rms(kv_lat, MLA_KV_LORA) * kvlg_ref[...]).astype(BF16), wkv_ref[...], preferred_element_type=F32)
    kpe = _rope(_rms(proj[:, C_KPE:C_KPE + LANES], MLA_ROPE) * kgr_ref[...], cosf, sinf).astype(BF16)
    q_scale = MLA_QK ** -0.5
    for hd in range(MLA_HEADS):
        qn = _rms(q[:, hd * 256:hd * 256 + 128], MLA_NOPE) * qgn_ref[...]
        qr = _rope(_rms(q[:, hd * 256 + 128:hd * 256 + 256], MLA_ROPE) * qgr_ref[...], cosf, sinf)
        qm_ref[0, hd, :, 0:128] = (qn * q_scale).astype(BF16)
        qm_ref[0, hd, :, 128:256] = (qr * q_scale).astype(BF16)
        kn = _rms(kv[:, hd * 256:hd * 256 + 128], MLA_NOPE) * kgn_ref[...]
        km_ref[0, hd, :, 0:128] = kn.astype(BF16)
        km_ref[0, hd, :, 128:256] = kpe
        vm_ref[0, hd] = kv[:, hd * 256 + 128:hd * 256 + 256].astype(BF16)

    lane = lax.broadcasted_iota(jnp.int32, (x.shape[0], LANES), 1)
    first = lane < DIFF_DQK

    def seg_norm(t):
        t2 = t * t
        s1 = jnp.sum(jnp.where(first, t2, 0.0), axis=-1, keepdims=True)
        s2 = jnp.sum(jnp.where(first, 0.0, t2), axis=-1, keepdims=True)
        r = jnp.where(first, lax.rsqrt(s1 * (1.0 / DIFF_DQK) + RMS_EPS), lax.rsqrt(s2 * (1.0 / DIFF_DQK) + RMS_EPS))
        return t * r

    d_scale = DIFF_DQK ** -0.5
    for hd in range(DIFF_HEADS):
        tq = _rope(seg_norm(proj[:, C_DQ + hd * 128:C_DQ + (hd + 1) * 128]) * dqg_ref[...], cosf, sinf) * d_scale
        qd_ref[0, hd, 0] = jnp.where(first, tq, 0.0).astype(BF16)
        qd_ref[0, hd, 1] = jnp.where(first, 0.0, tq).astype(BF16)
        tk = _rope(seg_norm(proj[:, C_DK + hd * 128:C_DK + (hd + 1) * 128]) * dkg_ref[...], cosf, sinf)
        kd_ref[0, hd] = tk.astype(BF16)
        vd_ref[0, hd] = proj[:, C_DV + hd * 128:C_DV + (hd + 1) * 128].astype(BF16)


def _proj(x, mod4, n1g, win_p, qlg, wq_p, kvlg, wkv, qgn, qgr, kgn, kgr, dqg, dkg, cosf, sinf):
    B, S, D = x.shape
    ts = min(PROJ_TS, S)
    row = lambda n: pl.BlockSpec((1, n), lambda b, s: (0, 0))
    full = lambda a: pl.BlockSpec(a.shape, lambda b, s: (0, 0))
    head_out = lambda w: pl.BlockSpec((1, MLA_HEADS, ts, w), lambda b, s: (b, 0, s, 0))
    return pl.pallas_call(
        _proj_kernel,
        grid=(B, S // ts),
        in_specs=[
            pl.BlockSpec((1, ts, D), lambda b, s: (b, s, 0)),
            pl.BlockSpec((1, 1, 1, D), lambda b, s: (b, 1, 0, 0)),
            pl.BlockSpec((1, 1, 1, D), lambda b, s: (b, 0, 0, 0)),
            row(D), full(win_p), row(MLA_Q_LORA), full(wq_p), row(MLA_KV_LORA), full(wkv),
            row(LANES), row(LANES), row(LANES), row(LANES), row(LANES), row(LANES),
            pl.BlockSpec((ts, LANES), lambda b, s: (s, 0)),
            pl.BlockSpec((ts, LANES), lambda b, s: (s, 0)),
        ],
        out_specs=[
            head_out(256), head_out(256), head_out(128),
            pl.BlockSpec((1, DIFF_HEADS, 2, ts, LANES), lambda b, s: (b, 0, 0, s, 0)),
            head_out(128), head_out(128),
        ],
        out_shape=[
            jax.ShapeDtypeStruct((B, MLA_HEADS, S, 256), BF16),
            jax.ShapeDtypeStruct((B, MLA_HEADS, S, 256), BF16),
            jax.ShapeDtypeStruct((B, MLA_HEADS, S, MLA_V), BF16),
            jax.ShapeDtypeStruct((B, DIFF_HEADS, 2, S, LANES), BF16),
            jax.ShapeDtypeStruct((B, DIFF_HEADS, S, LANES), BF16),
            jax.ShapeDtypeStruct((B, DIFF_HEADS, S, DIFF_DV), BF16),
        ],
        compiler_params=pltpu.CompilerParams(dimension_semantics=("arbitrary", "arbitrary"),
                                             vmem_limit_bytes=VMEM_LIMIT_BYTES),
        name="proj",
    )(x, mod4, mod4, n1g, win_p, qlg, wq_p, kvlg, wkv, qgn, qgr, kgn, kgr, dqg, dkg, cosf, sinf)


def _online_step(s, v, m, l, acc):
    m_new = jnp.maximum(m, jnp.max(s, axis=-1, keepdims=True))
    p = jnp.exp(s - m_new)
    alpha = jnp.exp(m - m_new)
    l = alpha * l + jnp.sum(p, axis=-1, keepdims=True)
    acc = alpha * acc + jnp.dot(p.astype(BF16), v, preferred_element_type=F32)
    return m_new, l, acc


_NT = (((1,), (1,)), ((), ()))


def _diag_mask(t):
    return lax.broadcasted_iota(jnp.int32, (t, t), 0) >= lax.broadcasted_iota(jnp.int32, (t, t), 1)


def _state(t, dv):
    return (jnp.full((t, 1), NEG_INF, F32), jnp.zeros((t, 1), F32), jnp.zeros((t, dv), F32))


def _mla_kernel(q_ref, k_ref, v_ref, o_ref, *, t):
    qi = pl.program_id(1)
    nh = q_ref.shape[1]
    qs = [q_ref[0, h] for h in range(nh)]
    mask = _diag_mask(t)

    def block(j, carry, diagonal):
        off = pl.multiple_of(j * t, t)
        out = []
        for h in range(nh):
            k = k_ref[0, h, pl.ds(off, t), :]
            v = v_ref[0, h, pl.ds(off, t), :]
            s = lax.dot_general(qs[h], k, _NT, preferred_element_type=F32)
            if diagonal:
                s = jnp.where(mask, s, NEG_INF)
            out.append(_online_step(s, v, *carry[h]))
        return tuple(out)

    carry = lax.fori_loop(0, qi, lambda j, c: block(j, c, False), tuple(_state(t, MLA_V) for _ in range(nh)))
    carry = block(qi, carry, True)
    for h in range(nh):
        _, l, acc = carry[h]
        o_ref[0, :, h * MLA_V:(h + 1) * MLA_V] = (acc / l).astype(o_ref.dtype)


def _diff_kernel(lam_ref, g_ref, q_ref, k_ref, v_ref, o_ref, *, t, out_scale):
    qi = pl.program_id(1)
    nh = q_ref.shape[1]
    qs = [(q_ref[0, h, 0], q_ref[0, h, 1]) for h in range(nh)]
    mask = _diag_mask(t)

    def block(j, carry, diagonal):
        off = pl.multiple_of(j * t, t)
        out = []
        for h in range(nh):
            k = k_ref[0, h, pl.ds(off, t), :]
            v = v_ref[0, h, pl.ds(off, t), :]
            for p in range(2):
                s = lax.dot_general(qs[h][p], k, _NT, preferred_element_type=F32)
                if diagonal:
                    s = jnp.where(mask, s, NEG_INF)
                out.append(_online_step(s, v, *carry[2 * h + p]))
        return tuple(out)

    carry = lax.fori_loop(0, qi, lambda j, c: block(j, c, False),
                          tuple(_state(t, DIFF_DV) for _ in range(2 * nh)))
    carry = block(qi, carry, True)
    for h in range(nh):
        _, l1, a1 = carry[2 * h]
        _, l2, a2 = carry[2 * h + 1]
        o = a1 / l1 - lam_ref[...] * (a2 / l2)
        o = _rms(o, DIFF_DV) * g_ref[...] * out_scale
        o_ref[0, :, h * DIFF_DV:(h + 1) * DIFF_DV] = o.astype(o_ref.dtype)


def _mla_attention(qm, km, vm):
    B, H, S, _ = qm.shape
    t = min(ATT_T, S)
    return pl.pallas_call(
        functools.partial(_mla_kernel, t=t),
        grid=(B, S // t),
        in_specs=[
            pl.BlockSpec((1, H, t, 256), lambda b, i: (b, 0, i, 0)),
            pl.BlockSpec((1, H, S, 256), lambda b, i: (b, 0, 0, 0)),
            pl.BlockSpec((1, H, S, MLA_V), lambda b, i: (b, 0, 0, 0)),
        ],
        out_specs=pl.BlockSpec((1, t, H * MLA_V), lambda b, i: (b, i, 0)),
        out_shape=jax.ShapeDtypeStruct((B, S, H * MLA_V), BF16),
        compiler_params=pltpu.CompilerParams(dimension_semantics=("arbitrary",) * 2,
                                             vmem_limit_bytes=VMEM_LIMIT_BYTES),
        name="mla_attn",
    )(qm, km, vm)


def _diff_attention(lam, subln_g, qd, kd, vd, out_scale):
    B, H, _, S, _ = qd.shape
    t = min(ATT_T, S)
    return pl.pallas_call(
        functools.partial(_diff_kernel, t=t, out_scale=out_scale),
        grid=(B, S // t),
        in_specs=[
            pl.BlockSpec((1, LANES), lambda b, i: (0, 0)),
            pl.BlockSpec((1, DIFF_DV), lambda b, i: (0, 0)),
            pl.BlockSpec((1, H, 2, t, LANES), lambda b, i: (b, 0, 0, i, 0)),
            pl.BlockSpec((1, H, S, LANES), lambda b, i: (b, 0, 0, 0)),
            pl.BlockSpec((1, H, S, DIFF_DV), lambda b, i: (b, 0, 0, 0)),
        ],
        out_specs=pl.BlockSpec((1, t, H * DIFF_DV), lambda b, i: (b, i, 0)),
        out_shape=jax.ShapeDtypeStruct((B, S, H * DIFF_DV), BF16),
        compiler_params=pltpu.CompilerParams(dimension_semantics=("arbitrary",) * 2,
                                             vmem_limit_bytes=VMEM_LIMIT_BYTES),
        name="diff_attn",
    )(lam, subln_g, qd, kd, vd)


def _topk_rows(vals, k, extra=None):
    n = vals.shape[0]
    row = lax.broadcasted_iota(jnp.int32, vals.shape, 0)
    out_v, out_i, out_e = [], [], []
    for _ in range(k):
        m = jnp.max(vals, axis=0, keepdims=True)
        sel = jnp.min(jnp.where(vals == m, row, n), axis=0, keepdims=True)
        hit = row == sel
        out_v.append(m)
        out_i.append(sel)
        if extra is not None:
            out_e.append(jnp.max(jnp.where(hit, extra, -1), axis=0, keepdims=True))
        vals = jnp.where(hit, NEG_INF, vals)
    cat = lambda xs: jnp.concatenate(xs, axis=0)
    return cat(out_v), cat(out_i), (cat(out_e) if extra is not None else None)


def _route_kernel(x_ref, mm_ref, md_ref, woa_ref, wob_ref, g1_ref, sc_ref, sh_ref, n2g_ref, wq_ref, keys_ref,
                  x1_ref, h2_ref, eidx_ref, gt_ref, st_scr, sv_scr, si_scr, gt_scr, et_scr):
    x = x_ref[0]
    D = x.shape[-1]
    o = jnp.dot(mm_ref[0], woa_ref[...], preferred_element_type=F32)
    o = o + jnp.dot(md_ref[0], wob_ref[...], preferred_element_type=F32)
    x1 = x + g1_ref[0, 0] * o
    x1_ref[0] = x1
    h2 = _rms(x1, D) * n2g_ref[...] * (1.0 + sc_ref[0, 0]) + sh_ref[0, 0]
    h2_ref[0] = h2
    q = jnp.dot(h2.astype(BF16), wq_ref[...], preferred_element_type=F32).astype(BF16)
    ngroups = 2 * PEER_HEADS
    for g in range(ngroups):
        st_scr[g] = lax.dot_general(keys_ref[g], q[:, g * PEER_DK_HALF:(g + 1) * PEER_DK_HALF], _NT,
                                    preferred_element_type=F32)

    def sub_topk(g, carry):
        v, i, _ = _topk_rows(st_scr[g], PEER_TOPK)
        sv_scr[g] = v
        si_scr[g] = i
        return carry

    lax.fori_loop(0, ngroups, sub_topk, 0)

    def head_topk(hd, carry):
        v0, v1 = sv_scr[2 * hd], sv_scr[2 * hd + 1]
        i0, i1 = si_scr[2 * hd], si_scr[2 * hd + 1]
        k = PEER_TOPK
        sub = lax.broadcasted_iota(jnp.int32, (8, v0.shape[1]), 0)
        cv, ci = [v0[0:1] + v1], [i0[0:1] * PEER_NKEYS + i1]
        for a in range(1, 8):
            cv.append(jnp.where(sub < k // (a + 1), v0[a:a + 1] + v1[0:8], NEG_INF))
            ci.append(i0[a:a + 1] * PEER_NKEYS + i1[0:8])
        cv.append(v0[8:k] + v1[0:1])
        ci.append(i0[8:k] * PEER_NKEYS + i1[0:1])
        fv, _, e = _topk_rows(jnp.concatenate(cv, axis=0), k, extra=jnp.concatenate(ci, axis=0))
        p = jnp.exp(fv - fv[0:1])
        gate = p / jnp.sum(p, axis=0, keepdims=True)
        r0 = pl.multiple_of(hd * PEER_TOPK, PEER_TOPK)
        et_scr[pl.ds(r0, PEER_TOPK), :] = e
        gt_scr[pl.ds(r0, PEER_TOPK), :] = gate
        return carry

    lax.fori_loop(0, PEER_HEADS, head_topk, 0)
    gt_ref[...] = gt_scr[...]
    eidx_ref[...] = et_scr[...].T


def _route(x, mixed_m, mixed_d, woa, wob, mod4, n2g, wq, keys):
    B, S, D = x.shape
    T = B * S
    ts = min(ROUTE_TS, S)
    nst = S // ts
    full = lambda a: pl.BlockSpec(a.shape, lambda b, s: (0,) * a.ndim)
    modspec = lambda k: pl.BlockSpec((1, 1, 1, D), lambda b, s: (b, k, 0, 0))
    tok = lambda w: pl.BlockSpec((1, ts, w), lambda b, s: (b, s, 0))
    return pl.pallas_call(
        _route_kernel,
        grid=(B, nst),
        in_specs=[
            tok(D), tok(mixed_m.shape[-1]), tok(mixed_d.shape[-1]), full(woa), full(wob),
            modspec(2), modspec(4), modspec(3),
            pl.BlockSpec((1, D), lambda b, s: (0, 0)), full(wq), full(keys),
        ],
        out_specs=[
            tok(D), tok(D),
            pl.BlockSpec((ts, PEER_NE), lambda b, s: (b * nst + s, 0)),
            pl.BlockSpec((PEER_NE, ts), lambda b, s: (0, b * nst + s)),
        ],
        out_shape=[
            jax.ShapeDtypeStruct((B, S, D), F32),
            jax.ShapeDtypeStruct((B, S, D), F32),
            jax.ShapeDtypeStruct((T, PEER_NE), jnp.int32),
            jax.ShapeDtypeStruct((PEER_NE, T), F32),
        ],
        scratch_shapes=[
            pltpu.VMEM((2 * PEER_HEADS, PEER_NKEYS, ts), F32),
            pltpu.VMEM((2 * PEER_HEADS, PEER_TOPK, ts), F32),
            pltpu.VMEM((2 * PEER_HEADS, PEER_TOPK, ts), jnp.int32),
            pltpu.VMEM((PEER_NE, ts), F32),
            pltpu.VMEM((PEER_NE, ts), jnp.int32),
        ],
        compiler_params=pltpu.CompilerParams(dimension_semantics=("arbitrary", "arbitrary"),
                                             vmem_limit_bytes=VMEM_LIMIT_BYTES),
        name="route",
    )(x, mixed_m, mixed_d, woa, wob, mod4, mod4, mod4, n2g, wq, keys)


_SQRT_HALF = float(np.sqrt(0.5))


def _gelu(a):
    return 0.5 * a * (1.0 + lax.erf(a * _SQRT_HALF))


def _mix_token(tile, hrow, gate, before_a=None, before_y=None):
    nchunk = hrow.shape[-1] // LANES
    ngrp = gate.shape[0] // 8
    acts = []
    for g in range(ngrp):
        if before_a is not None:
            before_a(g)
        acc = jnp.zeros((8, LANES), F32)
        for c in range(nchunk):
            u = lax.bitcast_convert_type(tile(g, c) << 16, F32)
            acc = acc + u * hrow[:, c * LANES:(c + 1) * LANES]
        a = jnp.sum(acc, axis=1, keepdims=True)
        acts.append(_gelu(a) * gate[g * 8:(g + 1) * 8])
    yacc = [jnp.zeros((8, LANES), F32) for _ in range(nchunk)]
    for g in range(ngrp):
        if before_y is not None:
            before_y(g)
        for c in range(nchunk):
            v = lax.bitcast_convert_type(tile(g, c) & jnp.uint32(0xFFFF0000), F32)
            yacc[c] = yacc[c] + v * acts[g]
    return jnp.concatenate([jnp.sum(yc, axis=0, keepdims=True) for yc in yacc], axis=1)


def _gate_column(gt_ref, token):
    lane = lax.broadcasted_iota(jnp.int32, gt_ref.shape, 1)
    return jnp.sum(jnp.where(lane == token, gt_ref[...], 0.0), axis=1, keepdims=True)


def _peer_kernel(idx_ref, idxn_ref, h_ref, gt_ref, x1_ref, g2_ref, tab_ref, o_ref, buf, sem, *, tt):
    ne = PEER_NE
    rt = PEER_ROW_TILE
    look = PEER_NSLOT - 1
    nchunk = h_ref.shape[-1] // LANES
    ngrp = ne // 8
    step = pl.program_id(0)
    nsteps = pl.num_programs(0)
    nparts = 2 * ngrp
    per = ne // nparts

    def issue_part(iref, t, slot, part):
        for e in range(part * per, (part + 1) * per):
            row = pl.multiple_of(iref[t, e] * rt, rt)
            g, r = divmod(e, 8)
            pltpu.make_async_copy(tab_ref.at[pl.ds(row, rt), :], buf.at[slot, pl.ds(g * nchunk, nchunk), r, :],
                                  sem.at[slot]).start(priority=e % 2)

    def wait(slot):
        pltpu.make_async_copy(buf.at[slot], buf.at[slot], sem.at[slot]).wait()

    @pl.when(step == 0)
    def _():
        for t in range(look):
            for part in range(nparts):
                issue_part(idx_ref, t, t, part)

    def group(base, nxt_of):
        h8 = h_ref[pl.ds(base, 8), :]
        ys = []
        for j in range(8):
            iref, tn = nxt_of(j)
            nslot = (j + look) % PEER_NSLOT
            slot = j % PEER_NSLOT
            wait(slot)
            ys.append(_mix_token(lambda g, c: buf[slot, g * nchunk + c], h8[j:j + 1], _gate_column(gt_ref, base + j),
                                 before_a=lambda g: issue_part(iref, tn, nslot, g),
                                 before_y=lambda g: issue_part(iref, tn, nslot, ngrp + g)))
        y = jnp.concatenate(ys, axis=0)
        o_ref[pl.ds(base, 8), :] = x1_ref[pl.ds(base, 8), :] + g2_ref[0, 0] * y

    def body(gi, carry):
        base = pl.multiple_of(gi * 8, 8)
        group(base, lambda j: (idx_ref, base + j + look))
        return carry

    lax.fori_loop(0, tt // 8 - 1, body, 0)
    last = tt - 8
    group(last, lambda j: (idx_ref, last + j + look) if j + look < 8 else (idxn_ref, j + look - 8))

    @pl.when(step == nsteps - 1)
    def _():
        for t in range(look):
            wait(t)


def _peer_gathered(eidx, h2, gt, x1, mod4, table, nblk, S):
    T, D = h2.shape
    tt = min(PEER_TT, S)
    nst = S // tt
    nchunk = D // LANES
    return pl.pallas_call(
        functools.partial(_peer_kernel, tt=tt),
        grid=(nblk,),
        in_specs=[
            pl.BlockSpec((tt, PEER_NE), lambda i: (i, 0), memory_space=pltpu.SMEM),
            pl.BlockSpec((tt, PEER_NE), lambda i: (jnp.minimum(i + 1, nblk - 1), 0), memory_space=pltpu.SMEM),
            pl.BlockSpec((tt, D), lambda i: (i, 0)),
            pl.BlockSpec((PEER_NE, tt), lambda i: (0, i)),
            pl.BlockSpec((tt, D), lambda i: (i, 0)),
            pl.BlockSpec((1, 1, 1, D), lambda i: (i // nst, 5, 0, 0)),
            pl.BlockSpec(memory_space=pl.ANY),
        ],
        out_specs=pl.BlockSpec((tt, D), lambda i: (i, 0)),
        out_shape=jax.ShapeDtypeStruct((nblk * tt, D), F32),
        scratch_shapes=[pltpu.VMEM((PEER_NSLOT, PEER_NE // 8 * nchunk, 8, LANES), jnp.uint32),
                        pltpu.SemaphoreType.DMA((PEER_NSLOT,))],
        compiler_params=pltpu.CompilerParams(dimension_semantics=("arbitrary",),
                                             vmem_limit_bytes=VMEM_LIMIT_BYTES),
        name="peer_gather",
    )(eidx, eidx, h2, gt, x1, mod4, table)


def _sc_gather(table, idx):
    nb = idx.shape[0]
    d = table.shape[1]
    per_w = nb // SC_WORKERS
    nchunks = per_w // SC_CHUNK
    mesh = plsc.VectorSubcoreMesh(core_axis_name="c", subcore_axis_name="s")

    @functools.partial(
        pl.kernel, mesh=mesh,
        out_type=jax.ShapeDtypeStruct((nb, d), table.dtype),
        scratch_types=[pltpu.VMEM((SC_CHUNK,), jnp.int32), pltpu.VMEM((SC_CHUNK, d), table.dtype),
                       pltpu.SemaphoreType.DMA],
    )
    def k(table_hbm, idx_hbm, out_hbm, idx_v, rows_v, sem):
        wid = lax.axis_index("s") * 2 + lax.axis_index("c")
        base = wid * per_w

        @pl.loop(0, nchunks)
        def _(i):
            off = pl.multiple_of(base + i * SC_CHUNK, SC_CHUNK)
            pltpu.sync_copy(idx_hbm.at[pl.ds(off, SC_CHUNK)], idx_v)
            pltpu.async_copy(table_hbm.at[idx_v], rows_v, sem).wait()
            pltpu.sync_copy(rows_v, out_hbm.at[pl.ds(off, SC_CHUNK)])

    return k(table, idx)


def _dense_kernel(rows_ref, h_ref, gt_ref, x1_ref, g2_ref, o_ref):
    ne = PEER_NE
    i = pl.program_id(0)
    per_blk = gt_ref.shape[1] // 8
    ys = []
    for j in range(8):
        tile = lambda g, c, j=j: rows_ref[j * ne + g * 8:j * ne + (g + 1) * 8, c * LANES:(c + 1) * LANES]
        ys.append(_mix_token(tile, h_ref[j:j + 1, :], _gate_column(gt_ref, (i % per_blk) * 8 + j)))
    o_ref[...] = x1_ref[...] + g2_ref[0, 0] * jnp.concatenate(ys, axis=0)


def _peer_dense(rows, h2, gt, x1, mod4, blk0, nblk, S):
    T, D = h2.shape
    tt = min(PEER_TT, S)
    nst = S // tt
    per_blk = tt // 8
    return pl.pallas_call(
        _dense_kernel,
        grid=(nblk * per_blk,),
        in_specs=[
            pl.BlockSpec((8 * PEER_NE, D), lambda i: (i, 0)),
            pl.BlockSpec((8, D), lambda i: (blk0 * per_blk + i, 0)),
            pl.BlockSpec((PEER_NE, tt), lambda i: (0, blk0 + i // per_blk)),
            pl.BlockSpec((8, D), lambda i: (blk0 * per_blk + i, 0)),
            pl.BlockSpec((1, 1, 1, D), lambda i: ((blk0 + i // per_blk) // nst, 5, 0, 0)),
        ],
        out_specs=pl.BlockSpec((8, D), lambda i: (i, 0)),
        out_shape=jax.ShapeDtypeStruct((nblk * tt, D), F32),
        compiler_params=pltpu.CompilerParams(dimension_semantics=("arbitrary",),
                                             vmem_limit_bytes=VMEM_LIMIT_BYTES),
        name="peer_dense",
    )(rows, h2, gt, x1, mod4)


def _rope_tables(S):
    half = MLA_ROPE // 2
    inv = 1.0 / (ROPE_THETA ** (jnp.arange(half, dtype=F32) / half))
    ang = jnp.arange(S, dtype=F32)[:, None] * inv[None, :]
    cos, sin = jnp.cos(ang), jnp.sin(ang)
    cosf = jnp.tile(cos, (1, 4))
    sinf = jnp.concatenate([-sin, sin, -sin, sin], axis=1)
    return cosf, sinf


def _pad_lanes(v, fill):
    return jnp.concatenate([v, jnp.full((LANES - v.shape[0],), fill, v.dtype)]).reshape(1, LANES)


def _pack_table(u, v):
    ub = lax.bitcast_convert_type(u.astype(BF16), jnp.uint16).astype(jnp.uint32)
    vb = lax.bitcast_convert_type(v.astype(BF16), jnp.uint16).astype(jnp.uint32)
    return ub | (vb << 16)


def kernel(x, c, ada_w, ada_b, norm1_g, w_in, mla_q_lat_g, mla_w_q_up, mla_kv_lat_g, mla_w_kv_up, mla_q_g, mla_k_g, diff_q_g, diff_k_g, diff_lq1, diff_lk1, diff_lq2, diff_lk2, diff_subln_g, w_out, norm2_g, peer_w_q, peer_sub_keys, peer_u, peer_v):
    B, S, D = x.shape
    depth = ada_w.shape[0]
    cosf, sinf = _rope_tables(S)
    o0 = MLA_Q_LORA
    o1 = o0 + MLA_KV_LORA
    o2 = o1 + MLA_ROPE
    nq = DIFF_HEADS * 2 * DIFF_DQK
    o3 = o2 + nq
    o4 = o3 + nq
    for l in range(depth):
        lambda_init = 0.8 - 0.6 * math.exp(-0.3 * l)
        mod, lam = _ada(c, ada_w[l], ada_b[l], diff_lq1[l], diff_lk1[l], diff_lq2[l], diff_lk2[l], lambda_init)
        mod4 = mod.reshape(B, 6, 1, D)

        wi = w_in[l]
        win_p = jnp.concatenate([wi[:, :o1], wi[:, o2:], wi[:, o1:o2], jnp.zeros((D, LANES - MLA_ROPE), F32)],
                                axis=1).astype(BF16)
        wq3 = mla_w_q_up[l].reshape(MLA_Q_LORA, MLA_HEADS, MLA_QK)
        wq_p = jnp.concatenate([wq3, jnp.zeros((MLA_Q_LORA, MLA_HEADS, 256 - MLA_QK), F32)], axis=2)
        wq_p = wq_p.reshape(MLA_Q_LORA, MLA_HEADS * 256).astype(BF16)
        qm, km, vm, qd, kd, vd = _proj(
            x, mod4, norm1_g[l].reshape(1, D), win_p, mla_q_lat_g[l].reshape(1, -1), wq_p,
            mla_kv_lat_g[l].reshape(1, -1), mla_w_kv_up[l].astype(BF16),
            mla_q_g[l, :MLA_NOPE].reshape(1, -1), _pad_lanes(mla_q_g[l, MLA_NOPE:], 1.0),
            mla_k_g[l, :MLA_NOPE].reshape(1, -1), _pad_lanes(mla_k_g[l, MLA_NOPE:], 1.0),
            jnp.tile(diff_q_g[l], 2).reshape(1, -1), jnp.tile(diff_k_g[l], 2).reshape(1, -1), cosf, sinf)

        mixed_m = _mla_attention(qm, km, vm)
        mixed_d = _diff_attention(lam, diff_subln_g[l].reshape(1, -1), qd, kd, vd, 1.0 - lambda_init)

        wo = w_out[l].astype(BF16)
        nm = MLA_HEADS * MLA_V
        keys = peer_sub_keys[l].reshape(2 * PEER_HEADS, PEER_NKEYS, PEER_DK_HALF).astype(BF16)
        x1, h2, eidx, gt = _route(x, mixed_m, mixed_d, wo[:nm], wo[nm:], mod4, norm2_g[l].reshape(1, D),
                                  peer_w_q[l].astype(BF16), keys)

        table = _pack_table(peer_u[l], peer_v[l])
        tt = min(PEER_TT, S)
        nblk = B * S // tt
        nsc = (nblk * SC_SHARE_NUM // SC_SHARE_DEN) if nblk >= SC_SHARE_DEN else 0
        ntc = nblk - nsc
        h2f, x1f = h2.reshape(B * S, D), x1.reshape(B * S, D)
        parts = [_peer_gathered(eidx, h2f, gt, x1f, mod4, table.reshape(-1, LANES), ntc, S)]
        if nsc:
            rows = _sc_gather(table, eidx[ntc * tt:].reshape(-1))
            parts.append(_peer_dense(rows, h2f, gt, x1f, mod4, ntc, nsc, S))
        x = jnp.concatenate(parts, axis=0).reshape(B, S, D)
    return x
```

```python
import functools
import math

import jax
import jax.numpy as jnp
import numpy as np
from jax import lax
from jax.experimental import pallas as pl
from jax.experimental.pallas import tpu as pltpu
from jax.experimental.pallas import tpu_sc as plsc

F32 = jnp.float32
BF16 = jnp.bfloat16

MLA_HEADS = 4
MLA_NOPE = 128
MLA_ROPE = 64
MLA_QK = MLA_NOPE + MLA_ROPE
MLA_V = 128
MLA_Q_LORA = 384
MLA_KV_LORA = 256
DIFF_HEADS = 4
DIFF_DQK = 64
DIFF_DV = 128
ROPE_THETA = 10000.0
RMS_EPS = 1e-6
PEER_HEADS = 8
PEER_NKEYS = 128
PEER_DK_HALF = 128
PEER_TOPK = 16
NEG_INF = float("-inf")

LANES = 128
VMEM_LIMIT_BYTES = 56 * 1024 * 1024

PROJ_TS = 512
ATT_T = 512
ROUTE_TS = 256
PEER_TT = 128
PEER_NSLOT = 8
PEER_NE = PEER_HEADS * PEER_TOPK
PEER_ROW_TILE = 8
SC_WORKERS = 32
SC_CHUNK = 64
DENSE_TOK = 16
SC_SHARE = ((5, 8), (5, 8), (5, 8), (9, 16))


def _rms(x, n):
    ss = jnp.sum(x * x, axis=-1, keepdims=True)
    return x * lax.rsqrt(ss * (1.0 / n) + RMS_EPS)


def _swap_halves64(x):
    lane = lax.broadcasted_iota(jnp.int32, x.shape, x.ndim - 1)
    first = (lane % 64) < 32
    return jnp.where(first, pltpu.roll(x, 96, x.ndim - 1), pltpu.roll(x, 32, x.ndim - 1))


def _rope(x, cosf, sinf):
    return x * cosf + _swap_halves64(x) * sinf


def _ada_kernel(c_ref, w_ref, b_ref, lq1_ref, lk1_ref, lq2_ref, lk2_ref, mod_ref, lam_ref, *, lambda_init):
    c = c_ref[...]
    s = c * jax.nn.sigmoid(c)
    mod_ref[...] = jnp.dot(s, w_ref[...], preferred_element_type=F32) + b_ref[...]
    d1 = jnp.sum(lq1_ref[...] * lk1_ref[...], axis=-1, keepdims=True)
    d2 = jnp.sum(lq2_ref[...] * lk2_ref[...], axis=-1, keepdims=True)
    lam = jnp.exp(d1) - jnp.exp(d2) + lambda_init
    lam_ref[...] = jnp.broadcast_to(lam, lam_ref.shape)


def _ada(c, ada_w, ada_b, lq1, lk1, lq2, lk2, lambda_init):
    B, D = c.shape
    N = ada_w.shape[1]
    bn = 1024
    small = pl.BlockSpec((1, DIFF_DQK), lambda j: (0, 0))
    return pl.pallas_call(
        functools.partial(_ada_kernel, lambda_init=lambda_init),
        grid=(N // bn,),
        in_specs=[
            pl.BlockSpec((B, D), lambda j: (0, 0)),
            pl.BlockSpec((D, bn), lambda j: (0, j)),
            pl.BlockSpec((1, bn), lambda j: (0, j)),
            small, small, small, small,
        ],
        out_specs=[pl.BlockSpec((B, bn), lambda j: (0, j)), pl.BlockSpec((1, LANES), lambda j: (0, 0))],
        out_shape=[jax.ShapeDtypeStruct((B, N), F32), jax.ShapeDtypeStruct((1, LANES), F32)],
        compiler_params=pltpu.CompilerParams(dimension_semantics=("arbitrary",)),
        name="ada",
    )(c, ada_w, ada_b.reshape(1, N), lq1.reshape(1, -1), lk1.reshape(1, -1), lq2.reshape(1, -1), lk2.reshape(1, -1))


C_QLAT = 0
C_KVLAT = C_QLAT + MLA_Q_LORA
C_DQ = C_KVLAT + MLA_KV_LORA
C_DK = C_DQ + DIFF_HEADS * 2 * DIFF_DQK
C_DV = C_DK + DIFF_HEADS * 2 * DIFF_DQK
C_KPE = C_DV + DIFF_HEADS * DIFF_DV
C_END = C_KPE + LANES


def _proj_kernel(x_ref, sc_ref, sh_ref, n1g_ref, win_ref, qlg_ref, wq_ref, kvlg_ref, wkv_ref,
                 qgn_ref, qgr_ref, kgn_ref, kgr_ref, dqg_ref, dkg_ref, cos_ref, sin_ref,
                 qm_ref, km_ref, vm_ref, qd_ref, kd_ref, vd_ref):
    x = x_ref[0]
    D = x.shape[-1]
    h = _rms(x, D) * n1g_ref[...] * (1.0 + sc_ref[0, 0]) + sh_ref[0, 0]
    proj = jnp.dot(h.astype(BF16), win_ref[...], preferred_element_type=F32)
    cosf = cos_ref[...]
    sinf = sin_ref[...]

    q_lat = proj[:, C_QLAT:C_QLAT + MLA_Q_LORA]
    q = jnp.dot((_rms(q_lat, MLA_Q_LORA) * qlg_ref[...]).astype(BF16), wq_ref[...], preferred_element_type=F32)
    kv_lat = proj[:, C_KVLAT:C_KVLAT + MLA_KV_LORA]
    kv = jnp.dot((_rms(kv_lat, MLA_KV_LORA) * kvlg_ref[...]).astype(BF16), wkv_ref[...], preferred_element_type=F32)
    kpe = _rope(_rms(proj[:, C_KPE:C_KPE + LANES], MLA_ROPE) * kgr_ref[...], cosf, sinf).astype(BF16)
    q_scale = MLA_QK ** -0.5
    for hd in range(MLA_HEADS):
        qn = _rms(q[:, hd * 256:hd * 256 + 128], MLA_NOPE) * qgn_ref[...]
        qr = _rope(_rms(q[:, hd * 256 + 128:hd * 256 + 256], MLA_ROPE) * qgr_ref[...], cosf, sinf)
        qm_ref[0, hd, :, 0:128] = (qn * q_scale).astype(BF16)
        qm_ref[0, hd, :, 128:256] = (qr * q_scale).astype(BF16)
        kn = _rms(kv[:, hd * 256:hd * 256 + 128], MLA_NOPE) * kgn_ref[...]
        km_ref[0, hd, :, 0:128] = kn.astype(BF16)
        km_ref[0, hd, :, 128:256] = kpe
        vm_ref[0, hd] = kv[:, hd * 256 + 128:hd * 256 + 256].astype(BF16)

    lane = lax.broadcasted_iota(jnp.int32, (x.shape[0], LANES), 1)
    first = lane < DIFF_DQK

    def seg_norm(t):
        t2 = t * t
        s1 = jnp.sum(jnp.where(first, t2, 0.0), axis=-1, keepdims=True)
        s2 = jnp.sum(jnp.where(first, 0.0, t2), axis=-1, keepdims=True)
        r = jnp.where(first, lax.rsqrt(s1 * (1.0 / DIFF_DQK) + RMS_EPS), lax.rsqrt(s2 * (1.0 / DIFF_DQK) + RMS_EPS))
        return t * r

    d_scale = DIFF_DQK ** -0.5
    for hd in range(DIFF_HEADS):
        tq = _rope(seg_norm(proj[:, C_DQ + hd * 128:C_DQ + (hd + 1) * 128]) * dqg_ref[...], cosf, sinf) * d_scale
        qd_ref[0, hd, 0] = jnp.where(first, tq, 0.0).astype(BF16)
        qd_ref[0, hd, 1] = jnp.where(first, 0.0, tq).astype(BF16)
        tk = _rope(seg_norm(proj[:, C_DK + hd * 128:C_DK + (hd + 1) * 128]) * dkg_ref[...], cosf, sinf)
        kd_ref[0, hd] = tk.astype(BF16)
        vd_ref[0, hd] = proj[:, C_DV + hd * 128:C_DV + (hd + 1) * 128].astype(BF16)


def _proj(x, b0, B, mod4, n1g, win_p, qlg, wq_p, kvlg, wkv, qgn, qgr, kgn, kgr, dqg, dkg, cosf, sinf):
    _, S, D = x.shape
    ts = min(PROJ_TS, S)
    row = lambda n: pl.BlockSpec((1, n), lambda b, s: (0, 0))
    full = lambda a: pl.BlockSpec(a.shape, lambda b, s: (0, 0))
    head_out = lambda w: pl.BlockSpec((1, MLA_HEADS, ts, w), lambda b, s: (b, 0, s, 0))
    return pl.pallas_call(
        _proj_kernel,
        grid=(B, S // ts),
        in_specs=[
            pl.BlockSpec((1, ts, D), lambda b, s: (b0 + b, s, 0)),
            pl.BlockSpec((1, 1, 1, D), lambda b, s: (b, 1, 0, 0)),
            pl.BlockSpec((1, 1, 1, D), lambda b, s: (b, 0, 0, 0)),
            row(D), full(win_p), row(MLA_Q_LORA), full(wq_p), row(MLA_KV_LORA), full(wkv),
            row(LANES), row(LANES), row(LANES), row(LANES), row(LANES), row(LANES),
            pl.BlockSpec((ts, LANES), lambda b, s: (s, 0)),
            pl.BlockSpec((ts, LANES), lambda b, s: (s, 0)),
        ],
        out_specs=[
            head_out(256), head_out(256), head_out(128),
            pl.BlockSpec((1, DIFF_HEADS, 2, ts, LANES), lambda b, s: (b, 0, 0, s, 0)),
            head_out(128), head_out(128),
        ],
        out_shape=[
            jax.ShapeDtypeStruct((B, MLA_HEADS, S, 256), BF16),
            jax.ShapeDtypeStruct((B, MLA_HEADS, S, 256), BF16),
            jax.ShapeDtypeStruct((B, MLA_HEADS, S, MLA_V), BF16),
            jax.ShapeDtypeStruct((B, DIFF_HEADS, 2, S, LANES), BF16),
            jax.ShapeDtypeStruct((B, DIFF_HEADS, S, LANES), BF16),
            jax.ShapeDtypeStruct((B, DIFF_HEADS, S, DIFF_DV), BF16),
        ],
        compiler_params=pltpu.CompilerParams(dimension_semantics=("arbitrary", "arbitrary"),
                                             vmem_limit_bytes=VMEM_LIMIT_BYTES),
        name="proj",
    )(x, mod4, mod4, n1g, win_p, qlg, wq_p, kvlg, wkv, qgn, qgr, kgn, kgr, dqg, dkg, cosf, sinf)


def _online_step(s, v, m, l, acc):
    m_new = jnp.maximum(m, jnp.max(s, axis=-1, keepdims=True))
    p = jnp.exp(s - m_new)
    alpha = jnp.exp(m - m_new)
    l = alpha * l + jnp.sum(p, axis=-1, keepdims=True)
    acc = alpha * acc + jnp.dot(p.astype(BF16), v, preferred_element_type=F32)
    return m_new, l, acc


_NT = (((1,), (1,)), ((), ()))


def _diag_mask(t):
    return lax.broadcasted_iota(jnp.int32, (t, t), 0) >= lax.broadcasted_iota(jnp.int32, (t, t), 1)


def _state(t, dv):
    return (jnp.full((t, 1), NEG_INF, F32), jnp.zeros((t, 1), F32), jnp.zeros((t, dv), F32))


def _mla_kernel(q_ref, k_ref, v_ref, o_ref, *, t):
    qi = pl.program_id(1)
    nh = q_ref.shape[1]
    qs = [q_ref[0, h] for h in range(nh)]
    mask = _diag_mask(t)

    def block(j, carry, diagonal):
        off = pl.multiple_of(j * t, t)
        ss = [lax.dot_general(qs[h], k_ref[0, h, pl.ds(off, t), :], _NT, preferred_element_type=F32)
              for h in range(nh)]
        if diagonal:
            ss = [jnp.where(mask, s, NEG_INF) for s in ss]
        return tuple(_online_step(ss[h], v_ref[0, h, pl.ds(off, t), :], *carry[h]) for h in range(nh))

    carry = lax.fori_loop(0, qi, lambda j, c: block(j, c, False), tuple(_state(t, MLA_V) for _ in range(nh)))
    carry = block(qi, carry, True)
    for h in range(nh):
        _, l, acc = carry[h]
        o_ref[0, :, h * MLA_V:(h + 1) * MLA_V] = (acc / l).astype(o_ref.dtype)


def _diff_kernel(lam_ref, g_ref, q_ref, k_ref, v_ref, o_ref, *, t, out_scale):
    qi = pl.program_id(1)
    nh = q_ref.shape[1]
    qs = [(q_ref[0, h, 0], q_ref[0, h, 1]) for h in range(nh)]
    mask = _diag_mask(t)

    def block(j, carry, diagonal):
        off = pl.multiple_of(j * t, t)
        out = []
        for hp in range(0, nh, 2):
            ss = [lax.dot_general(qs[h][p], k_ref[0, h, pl.ds(off, t), :], _NT, preferred_element_type=F32)
                  for h in (hp, hp + 1) for p in range(2)]
            if diagonal:
                ss = [jnp.where(mask, s, NEG_INF) for s in ss]
            for i, s in enumerate(ss):
                h = hp + i // 2
                out.append(_online_step(s, v_ref[0, h, pl.ds(off, t), :], *carry[2 * hp + i]))
        return tuple(out)

    carry = lax.fori_loop(0, qi, lambda j, c: block(j, c, False),
                          tuple(_state(t, DIFF_DV) for _ in range(2 * nh)))
    carry = block(qi, carry, True)
    for h in range(nh):
        _, l1, a1 = carry[2 * h]
        _, l2, a2 = carry[2 * h + 1]
        o = a1 / l1 - lam_ref[...] * (a2 / l2)
        o = _rms(o, DIFF_DV) * g_ref[...] * out_scale
        o_ref[0, :, h * DIFF_DV:(h + 1) * DIFF_DV] = o.astype(o_ref.dtype)


def _mla_attention(qm, km, vm):
    B, H, S, _ = qm.shape
    t = min(ATT_T, S)
    return pl.pallas_call(
        functools.partial(_mla_kernel, t=t),
        grid=(B, S // t),
        in_specs=[
            pl.BlockSpec((1, H, t, 256), lambda b, i: (b, 0, i, 0)),
            pl.BlockSpec((1, H, S, 256), lambda b, i: (b, 0, 0, 0)),
            pl.BlockSpec((1, H, S, MLA_V), lambda b, i: (b, 0, 0, 0)),
        ],
        out_specs=pl.BlockSpec((1, t, H * MLA_V), lambda b, i: (b, i, 0)),
        out_shape=jax.ShapeDtypeStruct((B, S, H * MLA_V), BF16),
        compiler_params=pltpu.CompilerParams(dimension_semantics=("arbitrary",) * 2,
                                             vmem_limit_bytes=VMEM_LIMIT_BYTES),
        name="mla_attn",
    )(qm, km, vm)


def _diff_attention(lam, subln_g, qd, kd, vd, out_scale):
    B, H, _, S, _ = qd.shape
    t = min(ATT_T, S)
    return pl.pallas_call(
        functools.partial(_diff_kernel, t=t, out_scale=out_scale),
        grid=(B, S // t),
        in_specs=[
            pl.BlockSpec((1, LANES), lambda b, i: (0, 0)),
            pl.BlockSpec((1, DIFF_DV), lambda b, i: (0, 0)),
            pl.BlockSpec((1, H, 2, t, LANES), lambda b, i: (b, 0, 0, i, 0)),
            pl.BlockSpec((1, H, S, LANES), lambda b, i: (b, 0, 0, 0)),
            pl.BlockSpec((1, H, S, DIFF_DV), lambda b, i: (b, 0, 0, 0)),
        ],
        out_specs=pl.BlockSpec((1, t, H * DIFF_DV), lambda b, i: (b, i, 0)),
        out_shape=jax.ShapeDtypeStruct((B, S, H * DIFF_DV), BF16),
        compiler_params=pltpu.CompilerParams(dimension_semantics=("arbitrary",) * 2,
                                             vmem_limit_bytes=VMEM_LIMIT_BYTES),
        name="diff_attn",
    )(lam, subln_g, qd, kd, vd)


def _topk_rows(vals, k, extra=None):
    n = vals.shape[0]
    row = lax.broadcasted_iota(jnp.int32, vals.shape, 0)
    out_v, out_i, out_e = [], [], []
    for _ in range(k):
        m = jnp.max(vals, axis=0, keepdims=True)
        sel = jnp.min(jnp.where(vals == m, row, n), axis=0, keepdims=True)
        hit = row == sel
        out_v.append(m)
        out_i.append(sel)
        if extra is not None:
            out_e.append(jnp.max(jnp.where(hit, extra, -1), axis=0, keepdims=True))
        vals = jnp.where(hit, NEG_INF, vals)
    cat = lambda xs: jnp.concatenate(xs, axis=0)
    return cat(out_v), cat(out_i), (cat(out_e) if extra is not None else None)


def _route_kernel(x_ref, mm_ref, md_ref, woa_ref, wob_ref, g1_ref, sc_ref, sh_ref, n2g_ref, wq_ref, keys_ref,
                  x1_ref, h2_ref, eidx_ref, gt_ref, st_scr, sv_scr, si_scr, gt_scr, et_scr):
    x = x_ref[0]
    D = x.shape[-1]
    o = jnp.dot(mm_ref[0], woa_ref[...], preferred_element_type=F32)
    o = o + jnp.dot(md_ref[0], wob_ref[...], preferred_element_type=F32)
    x1 = x + g1_ref[0, 0] * o
    x1_ref[0] = x1
    h2 = _rms(x1, D) * n2g_ref[...] * (1.0 + sc_ref[0, 0]) + sh_ref[0, 0]
    h2_ref[0] = h2
    q = jnp.dot(h2.astype(BF16), wq_ref[...], preferred_element_type=F32).astype(BF16)
    ngroups = 2 * PEER_HEADS
    for g in range(ngroups):
        st_scr[g] = lax.dot_general(keys_ref[g], q[:, g * PEER_DK_HALF:(g + 1) * PEER_DK_HALF], _NT,
                                    preferred_element_type=F32)

    def sub_topk(g, carry):
        v, i, _ = _topk_rows(st_scr[g], PEER_TOPK)
        sv_scr[g] = v
        si_scr[g] = i
        return carry

    lax.fori_loop(0, ngroups, sub_topk, 0)

    def head_topk(hd, carry):
        v0, v1 = sv_scr[2 * hd], sv_scr[2 * hd + 1]
        i0, i1 = si_scr[2 * hd], si_scr[2 * hd + 1]
        k = PEER_TOPK
        sub = lax.broadcasted_iota(jnp.int32, (8, v0.shape[1]), 0)
        cv, ci = [v0[0:1] + v1], [i0[0:1] * PEER_NKEYS + i1]
        for a in range(1, 8):
            cv.append(jnp.where(sub < k // (a + 1), v0[a:a + 1] + v1[0:8], NEG_INF))
            ci.append(i0[a:a + 1] * PEER_NKEYS + i1[0:8])
        cv.append(v0[8:k] + v1[0:1])
        ci.append(i0[8:k] * PEER_NKEYS + i1[0:1])
        fv, _, e = _topk_rows(jnp.concatenate(cv, axis=0), k, extra=jnp.concatenate(ci, axis=0))
        p = jnp.exp(fv - fv[0:1])
        gate = p / jnp.sum(p, axis=0, keepdims=True)
        r0 = pl.multiple_of(hd * PEER_TOPK, PEER_TOPK)
        et_scr[pl.ds(r0, PEER_TOPK), :] = e
        gt_scr[pl.ds(r0, PEER_TOPK), :] = gate
        return carry

    lax.fori_loop(0, PEER_HEADS, head_topk, 0)
    gt_ref[...] = gt_scr[...]
    eidx_ref[...] = et_scr[...].T


def _route(x, b0, B, mixed_m, mixed_d, woa, wob, mod4, n2g, wq, keys):
    _, S, D = x.shape
    T = B * S
    ts = min(ROUTE_TS, S)
    nst = S // ts
    full = lambda a: pl.BlockSpec(a.shape, lambda b, s: (0,) * a.ndim)
    modspec = lambda k: pl.BlockSpec((1, 1, 1, D), lambda b, s: (b, k, 0, 0))
    tok = lambda w: pl.BlockSpec((1, ts, w), lambda b, s: (b, s, 0))
    return pl.pallas_call(
        _route_kernel,
        grid=(B, nst),
        in_specs=[
            pl.BlockSpec((1, ts, D), lambda b, s: (b0 + b, s, 0)),
            tok(mixed_m.shape[-1]), tok(mixed_d.shape[-1]), full(woa), full(wob),
            modspec(2), modspec(4), modspec(3),
            pl.BlockSpec((1, D), lambda b, s: (0, 0)), full(wq), full(keys),
        ],
        out_specs=[
            tok(D), tok(D),
            pl.BlockSpec((ts, PEER_NE), lambda b, s: (b * nst + s, 0)),
            pl.BlockSpec((PEER_NE, ts), lambda b, s: (0, b * nst + s)),
        ],
        out_shape=[
            jax.ShapeDtypeStruct((B, S, D), F32),
            jax.ShapeDtypeStruct((B, S, D), F32),
            jax.ShapeDtypeStruct((T, PEER_NE), jnp.int32),
            jax.ShapeDtypeStruct((PEER_NE, T), F32),
        ],
        scratch_shapes=[
            pltpu.VMEM((2 * PEER_HEADS, PEER_NKEYS, ts), F32),
            pltpu.VMEM((2 * PEER_HEADS, PEER_TOPK, ts), F32),
            pltpu.VMEM((2 * PEER_HEADS, PEER_TOPK, ts), jnp.int32),
            pltpu.VMEM((PEER_NE, ts), F32),
            pltpu.VMEM((PEER_NE, ts), jnp.int32),
        ],
        compiler_params=pltpu.CompilerParams(dimension_semantics=("arbitrary", "arbitrary"),
                                             vmem_limit_bytes=VMEM_LIMIT_BYTES),
        name="route",
    )(x, mixed_m, mixed_d, woa, wob, mod4, mod4, mod4, n2g, wq, keys)


_SQRT_HALF = float(np.sqrt(0.5))


def _gelu(a):
    return 0.5 * a * (1.0 + lax.erf(a * _SQRT_HALF))


def _mix_token(tile, hrow, gate, before_a=None, before_y=None):
    nchunk = hrow.shape[-1] // LANES
    ngrp = gate.shape[0] // 8
    acts = []
    for g in range(ngrp):
        if before_a is not None:
            before_a(g)
        acc = jnp.zeros((8, LANES), F32)
        for c in range(nchunk):
            u = lax.bitcast_convert_type(tile(g, c) << 16, F32)
            acc = acc + u * hrow[:, c * LANES:(c + 1) * LANES]
        a = jnp.sum(acc, axis=1, keepdims=True)
        acts.append(_gelu(a) * gate[g * 8:(g + 1) * 8])
    yacc = [jnp.zeros((8, LANES), F32) for _ in range(nchunk)]
    for g in range(ngrp):
        if before_y is not None:
            before_y(g)
        for c in range(nchunk):
            v = lax.bitcast_convert_type(tile(g, c) & jnp.uint32(0xFFFF0000), F32)
            yacc[c] = yacc[c] + v * acts[g]
    return jnp.concatenate([jnp.sum(yc, axis=0, keepdims=True) for yc in yacc], axis=1)


def _gate_column(gt_ref, token):
    lane = lax.broadcasted_iota(jnp.int32, gt_ref.shape, 1)
    return jnp.sum(jnp.where(lane == token, gt_ref[...], 0.0), axis=1, keepdims=True)


def _peer_kernel(idx_ref, idxn_ref, h_ref, gt_ref, x1_ref, g2_ref, tab_ref, o_ref, buf, sem, *, tt):
    ne = PEER_NE
    rt = PEER_ROW_TILE
    look = PEER_NSLOT - 1
    nchunk = h_ref.shape[-1] // LANES
    ngrp = ne // 8
    step = pl.program_id(0)
    nsteps = pl.num_programs(0)
    nparts = 2 * ngrp
    per = ne // nparts

    def issue_part(iref, t, slot, part):
        for e in range(part * per, (part + 1) * per):
            row = pl.multiple_of(iref[t, e] * rt, rt)
            g, r = divmod(e, 8)
            pltpu.make_async_copy(tab_ref.at[pl.ds(row, rt), :], buf.at[slot, pl.ds(g * nchunk, nchunk), r, :],
                                  sem.at[slot]).start(priority=e % 2)

    def wait(slot):
        pltpu.make_async_copy(buf.at[slot], buf.at[slot], sem.at[slot]).wait()

    @pl.when(step == 0)
    def _():
        for t in range(look):
            for part in range(nparts):
                issue_part(idx_ref, t, t, part)

    def group(base, nxt_of):
        h8 = h_ref[pl.ds(base, 8), :]
        ys = []
        for j in range(8):
            iref, tn = nxt_of(j)
            nslot = (j + look) % PEER_NSLOT
            slot = j % PEER_NSLOT
            wait(slot)
            ys.append(_mix_token(lambda g, c: buf[slot, g * nchunk + c], h8[j:j + 1], _gate_column(gt_ref, base + j),
                                 before_a=lambda g: issue_part(iref, tn, nslot, g),
                                 before_y=lambda g: issue_part(iref, tn, nslot, ngrp + g)))
        y = jnp.concatenate(ys, axis=0)
        o_ref[pl.ds(base, 8), :] = x1_ref[pl.ds(base, 8), :] + g2_ref[0, 0] * y

    def body(gi, carry):
        base = pl.multiple_of(gi * 8, 8)
        group(base, lambda j: (idx_ref, base + j + look))
        return carry

    lax.fori_loop(0, tt // 8 - 1, body, 0)
    last = tt - 8
    group(last, lambda j: (idx_ref, last + j + look) if j + look < 8 else (idxn_ref, j + look - 8))

    @pl.when(step == nsteps - 1)
    def _():
        for t in range(look):
            wait(t)


def _peer_gathered(eidx, h2, gt, x1, mod4, table, nblk, S):
    T, D = h2.shape
    tt = min(PEER_TT, S)
    nst = S // tt
    nchunk = D // LANES
    return pl.pallas_call(
        functools.partial(_peer_kernel, tt=tt),
        grid=(nblk,),
        in_specs=[
            pl.BlockSpec((tt, PEER_NE), lambda i: (i, 0), memory_space=pltpu.SMEM),
            pl.BlockSpec((tt, PEER_NE), lambda i: (jnp.minimum(i + 1, nblk - 1), 0), memory_space=pltpu.SMEM),
            pl.BlockSpec((tt, D), lambda i: (i, 0)),
            pl.BlockSpec((PEER_NE, tt), lambda i: (0, i)),
            pl.BlockSpec((tt, D), lambda i: (i, 0)),
            pl.BlockSpec((1, 1, 1, D), lambda i: (i // nst, 5, 0, 0)),
            pl.BlockSpec(memory_space=pl.ANY),
        ],
        out_specs=pl.BlockSpec((tt, D), lambda i: (i, 0)),
        out_shape=jax.ShapeDtypeStruct((T, D), F32),
        scratch_shapes=[pltpu.VMEM((PEER_NSLOT, PEER_NE // 8 * nchunk, 8, LANES), jnp.uint32),
                        pltpu.SemaphoreType.DMA((PEER_NSLOT,))],
        compiler_params=pltpu.CompilerParams(dimension_semantics=("arbitrary",),
                                             vmem_limit_bytes=VMEM_LIMIT_BYTES),
        name="peer_gather",
    )(eidx, eidx, h2, gt, x1, mod4, table)


def _sc_gather(table, idx):
    nb = idx.shape[0]
    d = table.shape[1]
    per_w = nb // SC_WORKERS
    nchunks = per_w // SC_CHUNK
    mesh = plsc.VectorSubcoreMesh(core_axis_name="c", subcore_axis_name="s")

    @functools.partial(
        pl.kernel, mesh=mesh,
        out_type=jax.ShapeDtypeStruct((nb, d), table.dtype),
        scratch_types=[pltpu.VMEM((SC_CHUNK,), jnp.int32), pltpu.VMEM((SC_CHUNK, d), table.dtype),
                       pltpu.SemaphoreType.DMA],
    )
    def k(table_hbm, idx_hbm, out_hbm, idx_v, rows_v, sem):
        wid = lax.axis_index("s") * 2 + lax.axis_index("c")
        base = wid * per_w

        @pl.loop(0, nchunks)
        def _(i):
            off = pl.multiple_of(base + i * SC_CHUNK, SC_CHUNK)
            pltpu.sync_copy(idx_hbm.at[pl.ds(off, SC_CHUNK)], idx_v)
            pltpu.async_copy(table_hbm.at[idx_v], rows_v, sem).wait()
            pltpu.sync_copy(rows_v, out_hbm.at[pl.ds(off, SC_CHUNK)])

    return k(table, idx)


def _dense_kernel(rows_ref, h_ref, gt_ref, x1_ref, g2_ref, prev_ref, o_ref):
    ne = PEER_NE
    nd = h_ref.shape[0]
    i = pl.program_id(0)
    per_blk = gt_ref.shape[1] // nd
    ys = []
    for j in range(nd):
        tile = lambda g, c, j=j: rows_ref[j * ne + g * 8:j * ne + (g + 1) * 8, c * LANES:(c + 1) * LANES]
        ys.append(_mix_token(tile, h_ref[j:j + 1, :], _gate_column(gt_ref, (i % per_blk) * nd + j)))
    o_ref[...] = x1_ref[...] + g2_ref[0, 0] * jnp.concatenate(ys, axis=0)


def _peer_dense(rows, h2, gt, x1, mod4, out, blk0, nblk, S):
    T, D = h2.shape
    tt = min(PEER_TT, S)
    nst = S // tt
    nd = DENSE_TOK
    per_blk = tt // nd
    return pl.pallas_call(
        _dense_kernel,
        grid=(nblk * per_blk,),
        in_specs=[
            pl.BlockSpec((nd * PEER_NE, D), lambda i: (i, 0)),
            pl.BlockSpec((nd, D), lambda i: (blk0 * per_blk + i, 0)),
            pl.BlockSpec((PEER_NE, tt), lambda i: (0, blk0 + i // per_blk)),
            pl.BlockSpec((nd, D), lambda i: (blk0 * per_blk + i, 0)),
            pl.BlockSpec((1, 1, 1, D), lambda i: ((blk0 + i // per_blk) // nst, 5, 0, 0)),
            pl.BlockSpec(memory_space=pl.ANY),
        ],
        out_specs=pl.BlockSpec((nd, D), lambda i: (blk0 * per_blk + i, 0)),
        out_shape=jax.ShapeDtypeStruct((T, D), F32),
        input_output_aliases={5: 0},
        compiler_params=pltpu.CompilerParams(dimension_semantics=("arbitrary",),
                                             vmem_limit_bytes=VMEM_LIMIT_BYTES),
        name="peer_dense",
    )(rows, h2, gt, x1, mod4, out)


def _rope_tables(S):
    half = MLA_ROPE // 2
    inv = 1.0 / (ROPE_THETA ** (jnp.arange(half, dtype=F32) / half))
    ang = jnp.arange(S, dtype=F32)[:, None] * inv[None, :]
    cos, sin = jnp.cos(ang), jnp.sin(ang)
    cosf = jnp.tile(cos, (1, 4))
    sinf = jnp.concatenate([-sin, sin, -sin, sin], axis=1)
    return cosf, sinf


def _pad_lanes(v, fill):
    return jnp.concatenate([v, jnp.full((LANES - v.shape[0],), fill, v.dtype)]).reshape(1, LANES)


def _pack_table(u, v):
    ub = lax.bitcast_convert_type(u.astype(BF16), jnp.uint16).astype(jnp.uint32)
    vb = lax.bitcast_convert_type(v.astype(BF16), jnp.uint16).astype(jnp.uint32)
    return ub | (vb << 16)


def kernel(x, c, ada_w, ada_b, norm1_g, w_in, mla_q_lat_g, mla_w_q_up, mla_kv_lat_g, mla_w_kv_up, mla_q_g, mla_k_g, diff_q_g, diff_k_g, diff_lq1, diff_lk1, diff_lq2, diff_lk2, diff_subln_g, w_out, norm2_g, peer_w_q, peer_sub_keys, peer_u, peer_v):
    B, S, D = x.shape
    depth = ada_w.shape[0]
    cosf, sinf = _rope_tables(S)
    o0 = MLA_Q_LORA
    o1 = o0 + MLA_KV_LORA
    o2 = o1 + MLA_ROPE
    nq = DIFF_HEADS * 2 * DIFF_DQK
    o3 = o2 + nq
    o4 = o3 + nq
    for l in range(depth):
        lambda_init = 0.8 - 0.6 * math.exp(-0.3 * l)
        mod, lam = _ada(c, ada_w[l], ada_b[l], diff_lq1[l], diff_lk1[l], diff_lq2[l], diff_lk2[l], lambda_init)
        mod4 = mod.reshape(B, 6, 1, D)

        wi = w_in[l]
        win_p = jnp.concatenate([wi[:, :o1], wi[:, o2:], wi[:, o1:o2], jnp.zeros((D, LANES - MLA_ROPE), F32)],
                                axis=1).astype(BF16)
        wq3 = mla_w_q_up[l].reshape(MLA_Q_LORA, MLA_HEADS, MLA_QK)
        wq_p = jnp.concatenate([wq3, jnp.zeros((MLA_Q_LORA, MLA_HEADS, 256 - MLA_QK), F32)], axis=2)
        wq_p = wq_p.reshape(MLA_Q_LORA, MLA_HEADS * 256).astype(BF16)
        wkv = mla_w_kv_up[l].astype(BF16)
        wo = w_out[l].astype(BF16)
        nm = MLA_HEADS * MLA_V
        keys = peer_sub_keys[l].reshape(2 * PEER_HEADS, PEER_NKEYS, PEER_DK_HALF).astype(BF16)
        wpq = peer_w_q[l].astype(BF16)
        table = _pack_table(peer_u[l], peer_v[l])
        tab_tc = table.reshape(-1, LANES)

        def routed(b0, nb, m4):
            qm, km, vm, qd, kd, vd = _proj(
                x, b0, nb, m4, norm1_g[l].reshape(1, D), win_p, mla_q_lat_g[l].reshape(1, -1), wq_p,
                mla_kv_lat_g[l].reshape(1, -1), wkv,
                mla_q_g[l, :MLA_NOPE].reshape(1, -1), _pad_lanes(mla_q_g[l, MLA_NOPE:], 1.0),
                mla_k_g[l, :MLA_NOPE].reshape(1, -1), _pad_lanes(mla_k_g[l, MLA_NOPE:], 1.0),
                jnp.tile(diff_q_g[l], 2).reshape(1, -1), jnp.tile(diff_k_g[l], 2).reshape(1, -1), cosf, sinf)
            mixed_m = _mla_attention(qm, km, vm)
            mixed_d = _diff_attention(lam, diff_subln_g[l].reshape(1, -1), qd, kd, vd, 1.0 - lambda_init)
            return _route(x, b0, nb, mixed_m, mixed_d, wo[:nm], wo[nm:], m4, norm2_g[l].reshape(1, D), wpq, keys)

        ngrp = len(SC_SHARE) if B % len(SC_SHARE) == 0 else 1
        bg = B // ngrp
        tt = min(PEER_TT, S)
        nblk = bg * S // tt
        m4s = [mod4[g * bg:(g + 1) * bg] for g in range(ngrp)]
        rts = [routed(g * bg, bg, m4s[g]) for g in range(ngrp)]
        outs = []
        for g in range(ngrp):
            x1, h2, eidx, gt = rts[g]
            num, den = SC_SHARE[g if ngrp > 1 else -1]
            nsc = (nblk * num // den) if nblk >= den else 0
            ntc = nblk - nsc
            h2f, x1f = h2.reshape(bg * S, D), x1.reshape(bg * S, D)
            rows = _sc_gather(table, eidx[ntc * tt:].reshape(-1)) if nsc else None
            out = _peer_gathered(eidx, h2f, gt, x1f, m4s[g], tab_tc, ntc, S)
            if nsc:
                out = _peer_dense(rows, h2f, gt, x1f, m4s[g], out, ntc, nsc, S)
            outs.append(out.reshape(bg, S, D))
        x = jnp.concatenate(outs, axis=0) if ngrp > 1 else outs[0]
    return x
```

```python
import functools
import math

import jax
import jax.numpy as jnp
import numpy as np
from jax import lax
from jax.experimental import pallas as pl
from jax.experimental.pallas import tpu as pltpu
from jax.experimental.pallas import tpu_sc as plsc

F32 = jnp.float32
BF16 = jnp.bfloat16

MLA_HEADS = 4
MLA_NOPE = 128
MLA_ROPE = 64
MLA_QK = MLA_NOPE + MLA_ROPE
MLA_V = 128
MLA_Q_LORA = 384
MLA_KV_LORA = 256
DIFF_HEADS = 4
DIFF_DQK = 64
DIFF_DV = 128
ROPE_THETA = 10000.0
RMS_EPS = 1e-6
PEER_HEADS = 8
PEER_NKEYS = 128
PEER_DK_HALF = 128
PEER_TOPK = 16
NEG_INF = float("-inf")

LANES = 128
VMEM_LIMIT_BYTES = 56 * 1024 * 1024

PROJ_TS = 512
ATT_T = 512
ROUTE_TS = 256
PEER_TT = 128
PEER_NSLOT = 8
PEER_NE = PEER_HEADS * PEER_TOPK
PEER_ROW_TILE = 8
SC_WORKERS = 32
SC_CHUNK = 64
DENSE_TOK = 16
SEQ_GROUPS = ((2, 3, 4), (4, 11, 16), (5, 5, 8), (5, 9, 16))


def _rms(x, n):
    ss = jnp.sum(x * x, axis=-1, keepdims=True)
    return x * lax.rsqrt(ss * (1.0 / n) + RMS_EPS)


def _swap_halves64(x):
    lane = lax.broadcasted_iota(jnp.int32, x.shape, x.ndim - 1)
    first = (lane % 64) < 32
    return jnp.where(first, pltpu.roll(x, 96, x.ndim - 1), pltpu.roll(x, 32, x.ndim - 1))


def _rope(x, cosf, sinf):
    return x * cosf + _swap_halves64(x) * sinf


def _ada_kernel(c_ref, w_ref, b_ref, lq1_ref, lk1_ref, lq2_ref, lk2_ref, mod_ref, lam_ref, *, lambda_init):
    c = c_ref[...]
    s = c * jax.nn.sigmoid(c)
    mod_ref[...] = jnp.dot(s, w_ref[...], preferred_element_type=F32) + b_ref[...]
    d1 = jnp.sum(lq1_ref[...] * lk1_ref[...], axis=-1, keepdims=True)
    d2 = jnp.sum(lq2_ref[...] * lk2_ref[...], axis=-1, keepdims=True)
    lam = jnp.exp(d1) - jnp.exp(d2) + lambda_init
    lam_ref[...] = jnp.broadcast_to(lam, lam_ref.shape)


def _ada(c, ada_w, ada_b, lq1, lk1, lq2, lk2, lambda_init):
    B, D = c.shape
    N = ada_w.shape[1]
    bn = 1024
    small = pl.BlockSpec((1, DIFF_DQK), lambda j: (0, 0))
    return pl.pallas_call(
        functools.partial(_ada_kernel, lambda_init=lambda_init),
        grid=(N // bn,),
        in_specs=[
            pl.BlockSpec((B, D), lambda j: (0, 0)),
            pl.BlockSpec((D, bn), lambda j: (0, j)),
            pl.BlockSpec((1, bn), lambda j: (0, j)),
            small, small, small, small,
        ],
        out_specs=[pl.BlockSpec((B, bn), lambda j: (0, j)), pl.BlockSpec((1, LANES), lambda j: (0, 0))],
        out_shape=[jax.ShapeDtypeStruct((B, N), F32), jax.ShapeDtypeStruct((1, LANES), F32)],
        compiler_params=pltpu.CompilerParams(dimension_semantics=("arbitrary",)),
        name="ada",
    )(c, ada_w, ada_b.reshape(1, N), lq1.reshape(1, -1), lk1.reshape(1, -1), lq2.reshape(1, -1), lk2.reshape(1, -1))


C_QLAT = 0
C_KVLAT = C_QLAT + MLA_Q_LORA
C_DQ = C_KVLAT + MLA_KV_LORA
C_DK = C_DQ + DIFF_HEADS * 2 * DIFF_DQK
C_DV = C_DK + DIFF_HEADS * 2 * DIFF_DQK
C_KPE = C_DV + DIFF_HEADS * DIFF_DV
C_END = C_KPE + LANES


def _proj_kernel(x_ref, sc_ref, sh_ref, n1g_ref, win_ref, qlg_ref, wq_ref, kvlg_ref, wkv_ref,
                 qgn_ref, qgr_ref, kgn_ref, kgr_ref, dqg_ref, dkg_ref, cos_ref, sin_ref,
                 qm_ref, km_ref, vm_ref, qd_ref, kd_ref, vd_ref):
    x = x_ref[0]
    D = x.shape[-1]
    h = _rms(x, D) * n1g_ref[...] * (1.0 + sc_ref[0, 0]) + sh_ref[0, 0]
    proj = jnp.dot(h.astype(BF16), win_ref[...], preferred_element_type=F32)
    cosf = cos_ref[...]
    sinf = sin_ref[...]

    q_lat = proj[:, C_QLAT:C_QLAT + MLA_Q_LORA]
    q = jnp.dot((_rms(q_lat, MLA_Q_LORA) * qlg_ref[...]).astype(BF16), wq_ref[...], preferred_element_type=F32)
    kv_lat = proj[:, C_KVLAT:C_KVLAT + MLA_KV_LORA]
    kv = jnp.dot((_rms(kv_lat, MLA_KV_LORA) * kvlg_ref[...]).astype(BF16), wkv_ref[...], preferred_element_type=F32)
    kpe = _rope(_rms(proj[:, C_KPE:C_KPE + LANES], MLA_ROPE) * kgr_ref[...], cosf, sinf).astype(BF16)
    q_scale = MLA_QK ** -0.5
    for hd in range(MLA_HEADS):
        qn = _rms(q[:, hd * 256:hd * 256 + 128], MLA_NOPE) * qgn_ref[...]
        qr = _rope(_rms(q[:, hd * 256 + 128:hd * 256 + 256], MLA_ROPE) * qgr_ref[...], cosf, sinf)
        qm_ref[0, hd, :, 0:128] = (qn * q_scale).astype(BF16)
        qm_ref[0, hd, :, 128:256] = (qr * q_scale).astype(BF16)
        kn = _rms(kv[:, hd * 256:hd * 256 + 128], MLA_NOPE) * kgn_ref[...]
        km_ref[0, hd, :, 0:128] = kn.astype(BF16)
        km_ref[0, hd, :, 128:256] = kpe
        vm_ref[0, hd] = kv[:, hd * 256 + 128:hd * 256 + 256].astype(BF16)

    lane = lax.broadcasted_iota(jnp.int32, (x.shape[0], LANES), 1)
    first = lane < DIFF_DQK

    def seg_norm(t):
        t2 = t * t
        s1 = jnp.sum(jnp.where(first, t2, 0.0), axis=-1, keepdims=True)
        s2 = jnp.sum(jnp.where(first, 0.0, t2), axis=-1, keepdims=True)
        r = jnp.where(first, lax.rsqrt(s1 * (1.0 / DIFF_DQK) + RMS_EPS), lax.rsqrt(s2 * (1.0 / DIFF_DQK) + RMS_EPS))
        return t * r

    d_scale = DIFF_DQK ** -0.5
    for hd in range(DIFF_HEADS):
        tq = _rope(seg_norm(proj[:, C_DQ + hd * 128:C_DQ + (hd + 1) * 128]) * dqg_ref[...], cosf, sinf) * d_scale
        qd_ref[0, hd, 0] = jnp.where(first, tq, 0.0).astype(BF16)
        qd_ref[0, hd, 1] = jnp.where(first, 0.0, tq).astype(BF16)
        tk = _rope(seg_norm(proj[:, C_DK + hd * 128:C_DK + (hd + 1) * 128]) * dkg_ref[...], cosf, sinf)
        kd_ref[0, hd] = tk.astype(BF16)
        vd_ref[0, hd] = proj[:, C_DV + hd * 128:C_DV + (hd + 1) * 128].astype(BF16)


def _proj(x, b0, B, mod4, n1g, win_p, qlg, wq_p, kvlg, wkv, qgn, qgr, kgn, kgr, dqg, dkg, cosf, sinf):
    _, S, D = x.shape
    ts = min(PROJ_TS, S)
    row = lambda n: pl.BlockSpec((1, n), lambda b, s: (0, 0))
    full = lambda a: pl.BlockSpec(a.shape, lambda b, s: (0, 0))
    head_out = lambda w: pl.BlockSpec((1, MLA_HEADS, ts, w), lambda b, s: (b, 0, s, 0))
    return pl.pallas_call(
        _proj_kernel,
        grid=(B, S // ts),
        in_specs=[
            pl.BlockSpec((1, ts, D), lambda b, s: (b0 + b, s, 0)),
            pl.BlockSpec((1, 1, 1, D), lambda b, s: (b, 1, 0, 0)),
            pl.BlockSpec((1, 1, 1, D), lambda b, s: (b, 0, 0, 0)),
            row(D), full(win_p), row(MLA_Q_LORA), full(wq_p), row(MLA_KV_LORA), full(wkv),
            row(LANES), row(LANES), row(LANES), row(LANES), row(LANES), row(LANES),
            pl.BlockSpec((ts, LANES), lambda b, s: (s, 0)),
            pl.BlockSpec((ts, LANES), lambda b, s: (s, 0)),
        ],
        out_specs=[
            head_out(256), head_out(256), head_out(128),
            pl.BlockSpec((1, DIFF_HEADS, 2, ts, LANES), lambda b, s: (b, 0, 0, s, 0)),
            head_out(128), head_out(128),
        ],
        out_shape=[
            jax.ShapeDtypeStruct((B, MLA_HEADS, S, 256), BF16),
            jax.ShapeDtypeStruct((B, MLA_HEADS, S, 256), BF16),
            jax.ShapeDtypeStruct((B, MLA_HEADS, S, MLA_V), BF16),
            jax.ShapeDtypeStruct((B, DIFF_HEADS, 2, S, LANES), BF16),
            jax.ShapeDtypeStruct((B, DIFF_HEADS, S, LANES), BF16),
            jax.ShapeDtypeStruct((B, DIFF_HEADS, S, DIFF_DV), BF16),
        ],
        compiler_params=pltpu.CompilerParams(dimension_semantics=("arbitrary", "arbitrary"),
                                             vmem_limit_bytes=VMEM_LIMIT_BYTES),
        name="proj",
    )(x, mod4, mod4, n1g, win_p, qlg, wq_p, kvlg, wkv, qgn, qgr, kgn, kgr, dqg, dkg, cosf, sinf)


def _online_step(s, v, m, l, acc):
    m_new = jnp.maximum(m, jnp.max(s, axis=-1, keepdims=True))
    p = jnp.exp(s - m_new)
    alpha = jnp.exp(m - m_new)
    l = alpha * l + jnp.sum(p, axis=-1, keepdims=True)
    acc = alpha * acc + jnp.dot(p.astype(BF16), v, preferred_element_type=F32)
    return m_new, l, acc


_NT = (((1,), (1,)), ((), ()))


def _diag_mask(t):
    return lax.broadcasted_iota(jnp.int32, (t, t), 0) >= lax.broadcasted_iota(jnp.int32, (t, t), 1)


def _state(t, dv):
    return (jnp.full((t, 1), NEG_INF, F32), jnp.zeros((t, 1), F32), jnp.zeros((t, dv), F32))


def _mla_kernel(q_ref, k_ref, v_ref, o_ref, *, t):
    qi = pl.program_id(1)
    nh = q_ref.shape[1]
    qs = [q_ref[0, h] for h in range(nh)]
    mask = _diag_mask(t)

    def block(j, carry, diagonal):
        off = pl.multiple_of(j * t, t)
        ss = [lax.dot_general(qs[h], k_ref[0, h, pl.ds(off, t), :], _NT, preferred_element_type=F32)
              for h in range(nh)]
        if diagonal:
            ss = [jnp.where(mask, s, NEG_INF) for s in ss]
        return tuple(_online_step(ss[h], v_ref[0, h, pl.ds(off, t), :], *carry[h]) for h in range(nh))

    carry = lax.fori_loop(0, qi, lambda j, c: block(j, c, False), tuple(_state(t, MLA_V) for _ in range(nh)))
    carry = block(qi, carry, True)
    for h in range(nh):
        _, l, acc = carry[h]
        o_ref[0, :, h * MLA_V:(h + 1) * MLA_V] = (acc / l).astype(o_ref.dtype)


def _diff_kernel(lam_ref, g_ref, q_ref, k_ref, v_ref, o_ref, *, t, out_scale):
    qi = pl.program_id(1)
    nh = q_ref.shape[1]
    qs = [(q_ref[0, h, 0], q_ref[0, h, 1]) for h in range(nh)]
    mask = _diag_mask(t)

    def block(j, carry, diagonal):
        off = pl.multiple_of(j * t, t)
        out = []
        for hp in range(0, nh, 2):
            ss = [lax.dot_general(qs[h][p], k_ref[0, h, pl.ds(off, t), :], _NT, preferred_element_type=F32)
                  for h in (hp, hp + 1) for p in range(2)]
            if diagonal:
                ss = [jnp.where(mask, s, NEG_INF) for s in ss]
            for i, s in enumerate(ss):
                h = hp + i // 2
                out.append(_online_step(s, v_ref[0, h, pl.ds(off, t), :], *carry[2 * hp + i]))
        return tuple(out)

    carry = lax.fori_loop(0, qi, lambda j, c: block(j, c, False),
                          tuple(_state(t, DIFF_DV) for _ in range(2 * nh)))
    carry = block(qi, carry, True)
    for h in range(nh):
        _, l1, a1 = carry[2 * h]
        _, l2, a2 = carry[2 * h + 1]
        o = a1 / l1 - lam_ref[...] * (a2 / l2)
        o = _rms(o, DIFF_DV) * g_ref[...] * out_scale
        o_ref[0, :, h * DIFF_DV:(h + 1) * DIFF_DV] = o.astype(o_ref.dtype)


def _mla_attention(qm, km, vm):
    B, H, S, _ = qm.shape
    t = min(ATT_T, S)
    return pl.pallas_call(
        functools.partial(_mla_kernel, t=t),
        grid=(B, S // t),
        in_specs=[
            pl.BlockSpec((1, H, t, 256), lambda b, i: (b, 0, i, 0)),
            pl.BlockSpec((1, H, S, 256), lambda b, i: (b, 0, 0, 0)),
            pl.BlockSpec((1, H, S, MLA_V), lambda b, i: (b, 0, 0, 0)),
        ],
        out_specs=pl.BlockSpec((1, t, H * MLA_V), lambda b, i: (b, i, 0)),
        out_shape=jax.ShapeDtypeStruct((B, S, H * MLA_V), BF16),
        compiler_params=pltpu.CompilerParams(dimension_semantics=("arbitrary",) * 2,
                                             vmem_limit_bytes=VMEM_LIMIT_BYTES),
        name="mla_attn",
    )(qm, km, vm)


def _diff_attention(lam, subln_g, qd, kd, vd, out_scale):
    B, H, _, S, _ = qd.shape
    t = min(ATT_T, S)
    return pl.pallas_call(
        functools.partial(_diff_kernel, t=t, out_scale=out_scale),
        grid=(B, S // t),
        in_specs=[
            pl.BlockSpec((1, LANES), lambda b, i: (0, 0)),
            pl.BlockSpec((1, DIFF_DV), lambda b, i: (0, 0)),
            pl.BlockSpec((1, H, 2, t, LANES), lambda b, i: (b, 0, 0, i, 0)),
            pl.BlockSpec((1, H, S, LANES), lambda b, i: (b, 0, 0, 0)),
            pl.BlockSpec((1, H, S, DIFF_DV), lambda b, i: (b, 0, 0, 0)),
        ],
        out_specs=pl.BlockSpec((1, t, H * DIFF_DV), lambda b, i: (b, i, 0)),
        out_shape=jax.ShapeDtypeStruct((B, S, H * DIFF_DV), BF16),
        compiler_params=pltpu.CompilerParams(dimension_semantics=("arbitrary",) * 2,
                                             vmem_limit_bytes=VMEM_LIMIT_BYTES),
        name="diff_attn",
    )(lam, subln_g, qd, kd, vd)


def _topk_rows(vals, k, extra=None):
    n = vals.shape[0]
    row = lax.broadcasted_iota(jnp.int32, vals.shape, 0)
    out_v, out_i, out_e = [], [], []
    for _ in range(k):
        m = jnp.max(vals, axis=0, keepdims=True)
        sel = jnp.min(jnp.where(vals == m, row, n), axis=0, keepdims=True)
        hit = row == sel
        out_v.append(m)
        out_i.append(sel)
        if extra is not None:
            out_e.append(jnp.max(jnp.where(hit, extra, -1), axis=0, keepdims=True))
        vals = jnp.where(hit, NEG_INF, vals)
    cat = lambda xs: jnp.concatenate(xs, axis=0)
    return cat(out_v), cat(out_i), (cat(out_e) if extra is not None else None)


def _route_kernel(x_ref, mm_ref, md_ref, woa_ref, wob_ref, g1_ref, sc_ref, sh_ref, n2g_ref, wq_ref, keys_ref,
                  x1_ref, h2_ref, eidx_ref, gt_ref, st_scr, sv_scr, si_scr, gt_scr, et_scr):
    x = x_ref[0]
    D = x.shape[-1]
    o = jnp.dot(mm_ref[0], woa_ref[...], preferred_element_type=F32)
    o = o + jnp.dot(md_ref[0], wob_ref[...], preferred_element_type=F32)
    x1 = x + g1_ref[0, 0] * o
    x1_ref[0] = x1
    h2 = _rms(x1, D) * n2g_ref[...] * (1.0 + sc_ref[0, 0]) + sh_ref[0, 0]
    h2_ref[0] = h2
    q = jnp.dot(h2.astype(BF16), wq_ref[...], preferred_element_type=F32).astype(BF16)
    ngroups = 2 * PEER_HEADS
    for g in range(ngroups):
        st_scr[g] = lax.dot_general(keys_ref[g], q[:, g * PEER_DK_HALF:(g + 1) * PEER_DK_HALF], _NT,
                                    preferred_element_type=F32)

    def sub_topk(g, carry):
        v, i, _ = _topk_rows(st_scr[g], PEER_TOPK)
        sv_scr[g] = v
        si_scr[g] = i
        return carry

    lax.fori_loop(0, ngroups, sub_topk, 0)

    def head_topk(hd, carry):
        v0, v1 = sv_scr[2 * hd], sv_scr[2 * hd + 1]
        i0, i1 = si_scr[2 * hd], si_scr[2 * hd + 1]
        k = PEER_TOPK
        sub = lax.broadcasted_iota(jnp.int32, (8, v0.shape[1]), 0)
        cv, ci = [v0[0:1] + v1], [i0[0:1] * PEER_NKEYS + i1]
        for a in range(1, 8):
            cv.append(jnp.where(sub < k // (a + 1), v0[a:a + 1] + v1[0:8], NEG_INF))
            ci.append(i0[a:a + 1] * PEER_NKEYS + i1[0:8])
        cv.append(v0[8:k] + v1[0:1])
        ci.append(i0[8:k] * PEER_NKEYS + i1[0:1])
        fv, _, e = _topk_rows(jnp.concatenate(cv, axis=0), k, extra=jnp.concatenate(ci, axis=0))
        p = jnp.exp(fv - fv[0:1])
        gate = p / jnp.sum(p, axis=0, keepdims=True)
        r0 = pl.multiple_of(hd * PEER_TOPK, PEER_TOPK)
        et_scr[pl.ds(r0, PEER_TOPK), :] = e
        gt_scr[pl.ds(r0, PEER_TOPK), :] = gate
        return carry

    lax.fori_loop(0, PEER_HEADS, head_topk, 0)
    gt_ref[...] = gt_scr[...]
    eidx_ref[...] = et_scr[...].T


def _route(x, b0, B, mixed_m, mixed_d, woa, wob, mod4, n2g, wq, keys):
    _, S, D = x.shape
    T = B * S
    ts = min(ROUTE_TS, S)
    nst = S // ts
    full = lambda a: pl.BlockSpec(a.shape, lambda b, s: (0,) * a.ndim)
    modspec = lambda k: pl.BlockSpec((1, 1, 1, D), lambda b, s: (b, k, 0, 0))
    tok = lambda w: pl.BlockSpec((1, ts, w), lambda b, s: (b, s, 0))
    return pl.pallas_call(
        _route_kernel,
        grid=(B, nst),
        in_specs=[
            pl.BlockSpec((1, ts, D), lambda b, s: (b0 + b, s, 0)),
            tok(mixed_m.shape[-1]), tok(mixed_d.shape[-1]), full(woa), full(wob),
            modspec(2), modspec(4), modspec(3),
            pl.BlockSpec((1, D), lambda b, s: (0, 0)), full(wq), full(keys),
        ],
        out_specs=[
            tok(D), tok(D),
            pl.BlockSpec((ts, PEER_NE), lambda b, s: (b * nst + s, 0)),
            pl.BlockSpec((PEER_NE, ts), lambda b, s: (0, b * nst + s)),
        ],
        out_shape=[
            jax.ShapeDtypeStruct((B, S, D), F32),
            jax.ShapeDtypeStruct((B, S, D), F32),
            jax.ShapeDtypeStruct((T, PEER_NE), jnp.int32),
            jax.ShapeDtypeStruct((PEER_NE, T), F32),
        ],
        scratch_shapes=[
            pltpu.VMEM((2 * PEER_HEADS, PEER_NKEYS, ts), F32),
            pltpu.VMEM((2 * PEER_HEADS, PEER_TOPK, ts), F32),
            pltpu.VMEM((2 * PEER_HEADS, PEER_TOPK, ts), jnp.int32),
            pltpu.VMEM((PEER_NE, ts), F32),
            pltpu.VMEM((PEER_NE, ts), jnp.int32),
        ],
        compiler_params=pltpu.CompilerParams(dimension_semantics=("arbitrary", "arbitrary"),
                                             vmem_limit_bytes=VMEM_LIMIT_BYTES),
        name="route",
    )(x, mixed_m, mixed_d, woa, wob, mod4, mod4, mod4, n2g, wq, keys)


_SQRT_HALF = float(np.sqrt(0.5))


def _gelu(a):
    return 0.5 * a * (1.0 + lax.erf(a * _SQRT_HALF))


def _mix_token(tile, hrow, gate, before_a=None, before_y=None):
    nchunk = hrow.shape[-1] // LANES
    ngrp = gate.shape[0] // 8
    acts = []
    for g in range(ngrp):
        if before_a is not None:
            before_a(g)
        acc = jnp.zeros((8, LANES), F32)
        for c in range(nchunk):
            u = lax.bitcast_convert_type(tile(g, c) << 16, F32)
            acc = acc + u * hrow[:, c * LANES:(c + 1) * LANES]
        a = jnp.sum(acc, axis=1, keepdims=True)
        acts.append(_gelu(a) * gate[g * 8:(g + 1) * 8])
    yacc = [jnp.zeros((8, LANES), F32) for _ in range(nchunk)]
    for g in range(ngrp):
        if before_y is not None:
            before_y(g)
        for c in range(nchunk):
            v = lax.bitcast_convert_type(tile(g, c) & jnp.uint32(0xFFFF0000), F32)
            yacc[c] = yacc[c] + v * acts[g]
    return jnp.concatenate([jnp.sum(yc, axis=0, keepdims=True) for yc in yacc], axis=1)


def _gate_column(gt_ref, token):
    lane = lax.broadcasted_iota(jnp.int32, gt_ref.shape, 1)
    return jnp.sum(jnp.where(lane == token, gt_ref[...], 0.0), axis=1, keepdims=True)


def _peer_kernel(idx_ref, idxn_ref, h_ref, gt_ref, x1_ref, g2_ref, tab_ref, o_ref, buf, sem, *, tt):
    ne = PEER_NE
    rt = PEER_ROW_TILE
    look = PEER_NSLOT - 1
    nchunk = h_ref.shape[-1] // LANES
    ngrp = ne // 8
    step = pl.program_id(0)
    nsteps = pl.num_programs(0)
    nparts = 2 * ngrp
    per = ne // nparts

    def issue_part(iref, t, slot, part):
        for e in range(part * per, (part + 1) * per):
            row = pl.multiple_of(iref[t, e] * rt, rt)
            g, r = divmod(e, 8)
            pltpu.make_async_copy(tab_ref.at[pl.ds(row, rt), :], buf.at[slot, pl.ds(g * nchunk, nchunk), r, :],
                                  sem.at[slot]).start(priority=e % 2)

    def wait(slot):
        pltpu.make_async_copy(buf.at[slot], buf.at[slot], sem.at[slot]).wait()

    @pl.when(step == 0)
    def _():
        for t in range(look):
            for part in range(nparts):
                issue_part(idx_ref, t, t, part)

    def group(base, nxt_of):
        h8 = h_ref[pl.ds(base, 8), :]
        ys = []
        for j in range(8):
            iref, tn = nxt_of(j)
            nslot = (j + look) % PEER_NSLOT
            slot = j % PEER_NSLOT
            wait(slot)
            ys.append(_mix_token(lambda g, c: buf[slot, g * nchunk + c], h8[j:j + 1], _gate_column(gt_ref, base + j),
                                 before_a=lambda g: issue_part(iref, tn, nslot, g),
                                 before_y=lambda g: issue_part(iref, tn, nslot, ngrp + g)))
        y = jnp.concatenate(ys, axis=0)
        o_ref[pl.ds(base, 8), :] = x1_ref[pl.ds(base, 8), :] + g2_ref[0, 0] * y

    def body(gi, carry):
        base = pl.multiple_of(gi * 8, 8)
        group(base, lambda j: (idx_ref, base + j + look))
        return carry

    lax.fori_loop(0, tt // 8 - 1, body, 0)
    last = tt - 8
    group(last, lambda j: (idx_ref, last + j + look) if j + look < 8 else (idxn_ref, j + look - 8))

    @pl.when(step == nsteps - 1)
    def _():
        for t in range(look):
            wait(t)


def _peer_gathered(eidx, h2, gt, x1, mod4, table, nblk, S):
    T, D = h2.shape
    tt = min(PEER_TT, S)
    nst = S // tt
    nchunk = D // LANES
    return pl.pallas_call(
        functools.partial(_peer_kernel, tt=tt),
        grid=(nblk,),
        in_specs=[
            pl.BlockSpec((tt, PEER_NE), lambda i: (i, 0), memory_space=pltpu.SMEM),
            pl.BlockSpec((tt, PEER_NE), lambda i: (jnp.minimum(i + 1, nblk - 1), 0), memory_space=pltpu.SMEM),
            pl.BlockSpec((tt, D), lambda i: (i, 0)),
            pl.BlockSpec((PEER_NE, tt), lambda i: (0, i)),
            pl.BlockSpec((tt, D), lambda i: (i, 0)),
            pl.BlockSpec((1, 1, 1, D), lambda i: (i // nst, 5, 0, 0)),
            pl.BlockSpec(memory_space=pl.ANY),
        ],
        out_specs=pl.BlockSpec((tt, D), lambda i: (i, 0)),
        out_shape=jax.ShapeDtypeStruct((T, D), F32),
        scratch_shapes=[pltpu.VMEM((PEER_NSLOT, PEER_NE // 8 * nchunk, 8, LANES), jnp.uint32),
                        pltpu.SemaphoreType.DMA((PEER_NSLOT,))],
        compiler_params=pltpu.CompilerParams(dimension_semantics=("arbitrary",),
                                             vmem_limit_bytes=VMEM_LIMIT_BYTES),
        name="peer_gather",
    )(eidx, eidx, h2, gt, x1, mod4, table)


def _sc_gather(table, idx):
    nb = idx.shape[0]
    d = table.shape[1]
    per_w = nb // SC_WORKERS
    nchunks = per_w // SC_CHUNK
    mesh = plsc.VectorSubcoreMesh(core_axis_name="c", subcore_axis_name="s")

    @functools.partial(
        pl.kernel, mesh=mesh,
        out_type=jax.ShapeDtypeStruct((nb, d), table.dtype),
        scratch_types=[pltpu.VMEM((SC_CHUNK,), jnp.int32), pltpu.VMEM((SC_CHUNK, d), table.dtype),
                       pltpu.SemaphoreType.DMA],
    )
    def k(table_hbm, idx_hbm, out_hbm, idx_v, rows_v, sem):
        wid = lax.axis_index("s") * 2 + lax.axis_index("c")
        base = wid * per_w

        @pl.loop(0, nchunks)
        def _(i):
            off = pl.multiple_of(base + i * SC_CHUNK, SC_CHUNK)
            pltpu.sync_copy(idx_hbm.at[pl.ds(off, SC_CHUNK)], idx_v)
            pltpu.async_copy(table_hbm.at[idx_v], rows_v, sem).wait()
            pltpu.sync_copy(rows_v, out_hbm.at[pl.ds(off, SC_CHUNK)])

    return k(table, idx)


def _dense_kernel(rows_ref, h_ref, gt_ref, x1_ref, g2_ref, prev_ref, o_ref):
    ne = PEER_NE
    nd = h_ref.shape[0]
    i = pl.program_id(0)
    per_blk = gt_ref.shape[1] // nd
    ys = []
    for j in range(nd):
        tile = lambda g, c, j=j: rows_ref[j * ne + g * 8:j * ne + (g + 1) * 8, c * LANES:(c + 1) * LANES]
        ys.append(_mix_token(tile, h_ref[j:j + 1, :], _gate_column(gt_ref, (i % per_blk) * nd + j)))
    o_ref[...] = x1_ref[...] + g2_ref[0, 0] * jnp.concatenate(ys, axis=0)


def _peer_dense(rows, h2, gt, x1, mod4, out, blk0, nblk, S):
    T, D = h2.shape
    tt = min(PEER_TT, S)
    nst = S // tt
    nd = DENSE_TOK
    per_blk = tt // nd
    return pl.pallas_call(
        _dense_kernel,
        grid=(nblk * per_blk,),
        in_specs=[
            pl.BlockSpec((nd * PEER_NE, D), lambda i: (i, 0)),
            pl.BlockSpec((nd, D), lambda i: (blk0 * per_blk + i, 0)),
            pl.BlockSpec((PEER_NE, tt), lambda i: (0, blk0 + i // per_blk)),
            pl.BlockSpec((nd, D), lambda i: (blk0 * per_blk + i, 0)),
            pl.BlockSpec((1, 1, 1, D), lambda i: ((blk0 + i // per_blk) // nst, 5, 0, 0)),
            pl.BlockSpec(memory_space=pl.ANY),
        ],
        out_specs=pl.BlockSpec((nd, D), lambda i: (blk0 * per_blk + i, 0)),
        out_shape=jax.ShapeDtypeStruct((T, D), F32),
        input_output_aliases={5: 0},
        compiler_params=pltpu.CompilerParams(dimension_semantics=("arbitrary",),
                                             vmem_limit_bytes=VMEM_LIMIT_BYTES),
        name="peer_dense",
    )(rows, h2, gt, x1, mod4, out)


def _rope_tables(S):
    half = MLA_ROPE // 2
    inv = 1.0 / (ROPE_THETA ** (jnp.arange(half, dtype=F32) / half))
    ang = jnp.arange(S, dtype=F32)[:, None] * inv[None, :]
    cos, sin = jnp.cos(ang), jnp.sin(ang)
    cosf = jnp.tile(cos, (1, 4))
    sinf = jnp.concatenate([-sin, sin, -sin, sin], axis=1)
    return cosf, sinf


def _pad_lanes(v, fill):
    return jnp.concatenate([v, jnp.full((LANES - v.shape[0],), fill, v.dtype)]).reshape(1, LANES)


def _pack_table(u, v):
    ub = lax.bitcast_convert_type(u.astype(BF16), jnp.uint16).astype(jnp.uint32)
    vb = lax.bitcast_convert_type(v.astype(BF16), jnp.uint16).astype(jnp.uint32)
    return ub | (vb << 16)


def kernel(x, c, ada_w, ada_b, norm1_g, w_in, mla_q_lat_g, mla_w_q_up, mla_kv_lat_g, mla_w_kv_up, mla_q_g, mla_k_g, diff_q_g, diff_k_g, diff_lq1, diff_lk1, diff_lq2, diff_lk2, diff_subln_g, w_out, norm2_g, peer_w_q, peer_sub_keys, peer_u, peer_v):
    B, S, D = x.shape
    depth = ada_w.shape[0]
    cosf, sinf = _rope_tables(S)
    o0 = MLA_Q_LORA
    o1 = o0 + MLA_KV_LORA
    o2 = o1 + MLA_ROPE
    nq = DIFF_HEADS * 2 * DIFF_DQK
    o3 = o2 + nq
    o4 = o3 + nq
    for l in range(depth):
        lambda_init = 0.8 - 0.6 * math.exp(-0.3 * l)
        mod, lam = _ada(c, ada_w[l], ada_b[l], diff_lq1[l], diff_lk1[l], diff_lq2[l], diff_lk2[l], lambda_init)
        mod4 = mod.reshape(B, 6, 1, D)

        wi = w_in[l]
        win_p = jnp.concatenate([wi[:, :o1], wi[:, o2:], wi[:, o1:o2], jnp.zeros((D, LANES - MLA_ROPE), F32)],
                                axis=1).astype(BF16)
        wq3 = mla_w_q_up[l].reshape(MLA_Q_LORA, MLA_HEADS, MLA_QK)
        wq_p = jnp.concatenate([wq3, jnp.zeros((MLA_Q_LORA, MLA_HEADS, 256 - MLA_QK), F32)], axis=2)
        wq_p = wq_p.reshape(MLA_Q_LORA, MLA_HEADS * 256).astype(BF16)
        wkv = mla_w_kv_up[l].astype(BF16)
        wo = w_out[l].astype(BF16)
        nm = MLA_HEADS * MLA_V
        keys = peer_sub_keys[l].reshape(2 * PEER_HEADS, PEER_NKEYS, PEER_DK_HALF).astype(BF16)
        wpq = peer_w_q[l].astype(BF16)
        table = _pack_table(peer_u[l], peer_v[l])
        tab_tc = table.reshape(-1, LANES)

        def routed(b0, nb, m4):
            qm, km, vm, qd, kd, vd = _proj(
                x, b0, nb, m4, norm1_g[l].reshape(1, D), win_p, mla_q_lat_g[l].reshape(1, -1), wq_p,
                mla_kv_lat_g[l].reshape(1, -1), wkv,
                mla_q_g[l, :MLA_NOPE].reshape(1, -1), _pad_lanes(mla_q_g[l, MLA_NOPE:], 1.0),
                mla_k_g[l, :MLA_NOPE].reshape(1, -1), _pad_lanes(mla_k_g[l, MLA_NOPE:], 1.0),
                jnp.tile(diff_q_g[l], 2).reshape(1, -1), jnp.tile(diff_k_g[l], 2).reshape(1, -1), cosf, sinf)
            mixed_m = _mla_attention(qm, km, vm)
            mixed_d = _diff_attention(lam, diff_subln_g[l].reshape(1, -1), qd, kd, vd, 1.0 - lambda_init)
            return _route(x, b0, nb, mixed_m, mixed_d, wo[:nm], wo[nm:], m4, norm2_g[l].reshape(1, D), wpq, keys)

        groups = SEQ_GROUPS if sum(g[0] for g in SEQ_GROUPS) == B else ((B,) + SEQ_GROUPS[0][1:],)
        tt = min(PEER_TT, S)
        starts = [sum(g[0] for g in groups[:i]) for i in range(len(groups))]
        m4s = [mod4[b0:b0 + g[0]] for b0, g in zip(starts, groups)]
        rts = [routed(b0, g[0], m4) for b0, g, m4 in zip(starts, groups, m4s)]
        outs = []
        for (nb, num, den), m4, (x1, h2, eidx, gt) in zip(groups, m4s, rts):
            nblk = nb * S // tt
            nsc = (nblk * num // den) if nblk >= den else 0
            ntc = nblk - nsc
            h2f, x1f = h2.reshape(nb * S, D), x1.reshape(nb * S, D)
            rows = _sc_gather(table, eidx[ntc * tt:].reshape(-1)) if nsc else None
            out = _peer_gathered(eidx, h2f, gt, x1f, m4, tab_tc, ntc, S)
            if nsc:
                out = _peer_dense(rows, h2f, gt, x1f, m4, out, ntc, nsc, S)
            outs.append(out.reshape(nb, S, D))
        x = jnp.concatenate(outs, axis=0) if len(outs) > 1 else outs[0]
    return x
```

```python
import functools
import math

import jax
import jax.numpy as jnp
import numpy as np
from jax import lax
from jax.experimental import pallas as pl
from jax.experimental.pallas import tpu as pltpu
from jax.experimental.pallas import tpu_sc as plsc

F32 = jnp.float32
BF16 = jnp.bfloat16

MLA_HEADS = 4
MLA_NOPE = 128
MLA_ROPE = 64
MLA_QK = MLA_NOPE + MLA_ROPE
MLA_V = 128
MLA_Q_LORA = 384
MLA_KV_LORA = 256
DIFF_HEADS = 4
DIFF_DQK = 64
DIFF_DV = 128
ROPE_THETA = 10000.0
RMS_EPS = 1e-6
PEER_HEADS = 8
PEER_NKEYS = 128
PEER_DK_HALF = 128
PEER_TOPK = 16
NEG_INF = float("-inf")

LANES = 128
VMEM_LIMIT_BYTES = 56 * 1024 * 1024

PROJ_TS = 512
ATT_T = 512
ROUTE_TS = 256
PEER_TT = 128
PEER_NSLOT = 8
PEER_NE = PEER_HEADS * PEER_TOPK
PEER_ROW_TILE = 8
SC_WORKERS = 32
SC_CHUNK = 64
DENSE_TOK = 16
SEQ_GROUPS = ((2, 3, 4), (4, 11, 16), (5, 5, 8), (5, 9, 16))


def _rms(x, n):
    ss = jnp.sum(x * x, axis=-1, keepdims=True)
    return x * lax.rsqrt(ss * (1.0 / n) + RMS_EPS)


def _swap_halves64(x):
    lane = lax.broadcasted_iota(jnp.int32, x.shape, x.ndim - 1)
    first = (lane % 64) < 32
    return jnp.where(first, pltpu.roll(x, 96, x.ndim - 1), pltpu.roll(x, 32, x.ndim - 1))


def _rope(x, cosf, sinf):
    return x * cosf + _swap_halves64(x) * sinf


def _ada_kernel(c_ref, w_ref, b_ref, lq1_ref, lk1_ref, lq2_ref, lk2_ref, mod_ref, lam_ref, *, lambda_init):
    c = c_ref[...]
    s = c * jax.nn.sigmoid(c)
    mod_ref[...] = jnp.dot(s, w_ref[...], preferred_element_type=F32) + b_ref[...]
    d1 = jnp.sum(lq1_ref[...] * lk1_ref[...], axis=-1, keepdims=True)
    d2 = jnp.sum(lq2_ref[...] * lk2_ref[...], axis=-1, keepdims=True)
    lam = jnp.exp(d1) - jnp.exp(d2) + lambda_init
    lam_ref[...] = jnp.broadcast_to(lam, lam_ref.shape)


def _ada(c, ada_w, ada_b, lq1, lk1, lq2, lk2, lambda_init):
    B, D = c.shape
    N = ada_w.shape[1]
    bn = 1024
    small = pl.BlockSpec((1, DIFF_DQK), lambda j: (0, 0))
    return pl.pallas_call(
        functools.partial(_ada_kernel, lambda_init=lambda_init),
        grid=(N // bn,),
        in_specs=[
            pl.BlockSpec((B, D), lambda j: (0, 0)),
            pl.BlockSpec((D, bn), lambda j: (0, j)),
            pl.BlockSpec((1, bn), lambda j: (0, j)),
            small, small, small, small,
        ],
        out_specs=[pl.BlockSpec((B, bn), lambda j: (0, j)), pl.BlockSpec((1, LANES), lambda j: (0, 0))],
        out_shape=[jax.ShapeDtypeStruct((B, N), F32), jax.ShapeDtypeStruct((1, LANES), F32)],
        compiler_params=pltpu.CompilerParams(dimension_semantics=("arbitrary",)),
        name="ada",
    )(c, ada_w, ada_b.reshape(1, N), lq1.reshape(1, -1), lk1.reshape(1, -1), lq2.reshape(1, -1), lk2.reshape(1, -1))


C_QLAT = 0
C_KVLAT = C_QLAT + MLA_Q_LORA
C_DQ = C_KVLAT + MLA_KV_LORA
C_DK = C_DQ + DIFF_HEADS * 2 * DIFF_DQK
C_DV = C_DK + DIFF_HEADS * 2 * DIFF_DQK
C_KPE = C_DV + DIFF_HEADS * DIFF_DV
C_END = C_KPE + LANES


def _proj_kernel(x_ref, sc_ref, sh_ref, n1g_ref, win_ref, qlg_ref, wq_ref, kvlg_ref, wkv_ref,
                 qgn_ref, qgr_ref, kgn_ref, kgr_ref, dqg_ref, dkg_ref, cos_ref, sin_ref,
                 qm_ref, km_ref, vm_ref, qd_ref, kd_ref, vd_ref):
    x = x_ref[0]
    D = x.shape[-1]
    h = _rms(x, D) * n1g_ref[...] * (1.0 + sc_ref[0, 0]) + sh_ref[0, 0]
    proj = jnp.dot(h.astype(BF16), win_ref[...], preferred_element_type=F32)
    cosf = cos_ref[...]
    sinf = sin_ref[...]

    q_lat = proj[:, C_QLAT:C_QLAT + MLA_Q_LORA]
    q = jnp.dot((_rms(q_lat, MLA_Q_LORA) * qlg_ref[...]).astype(BF16), wq_ref[...], preferred_element_type=F32)
    kv_lat = proj[:, C_KVLAT:C_KVLAT + MLA_KV_LORA]
    kv = jnp.dot((_rms(kv_lat, MLA_KV_LORA) * kvlg_ref[...]).astype(BF16), wkv_ref[...], preferred_element_type=F32)
    kpe = _rope(_rms(proj[:, C_KPE:C_KPE + LANES], MLA_ROPE) * kgr_ref[...], cosf, sinf).astype(BF16)
    q_scale = MLA_QK ** -0.5
    for hd in range(MLA_HEADS):
        qn = _rms(q[:, hd * 256:hd * 256 + 128], MLA_NOPE) * qgn_ref[...]
        qr = _rope(_rms(q[:, hd * 256 + 128:hd * 256 + 256], MLA_ROPE) * qgr_ref[...], cosf, sinf)
        qm_ref[0, hd, :, 0:128] = (qn * q_scale).astype(BF16)
        qm_ref[0, hd, :, 128:256] = (qr * q_scale).astype(BF16)
        kn = _rms(kv[:, hd * 256:hd * 256 + 128], MLA_NOPE) * kgn_ref[...]
        km_ref[0, hd, :, 0:128] = kn.astype(BF16)
        km_ref[0, hd, :, 128:256] = kpe
        vm_ref[0, hd] = kv[:, hd * 256 + 128:hd * 256 + 256].astype(BF16)

    lane = lax.broadcasted_iota(jnp.int32, (x.shape[0], LANES), 1)
    first = lane < DIFF_DQK

    def seg_norm(t):
        t2 = t * t
        s1 = jnp.sum(jnp.where(first, t2, 0.0), axis=-1, keepdims=True)
        s2 = jnp.sum(jnp.where(first, 0.0, t2), axis=-1, keepdims=True)
        r = jnp.where(first, lax.rsqrt(s1 * (1.0 / DIFF_DQK) + RMS_EPS), lax.rsqrt(s2 * (1.0 / DIFF_DQK) + RMS_EPS))
        return t * r

    d_scale = DIFF_DQK ** -0.5
    for hd in range(DIFF_HEADS):
        tq = _rope(seg_norm(proj[:, C_DQ + hd * 128:C_DQ + (hd + 1) * 128]) * dqg_ref[...], cosf, sinf) * d_scale
        qd_ref[0, hd, 0] = jnp.where(first, tq, 0.0).astype(BF16)
        qd_ref[0, hd, 1] = jnp.where(first, 0.0, tq).astype(BF16)
        tk = _rope(seg_norm(proj[:, C_DK + hd * 128:C_DK + (hd + 1) * 128]) * dkg_ref[...], cosf, sinf)
        kd_ref[0, hd] = tk.astype(BF16)
        vd_ref[0, hd] = proj[:, C_DV + hd * 128:C_DV + (hd + 1) * 128].astype(BF16)


def _proj(x, b0, B, mod4, n1g, win_p, qlg, wq_p, kvlg, wkv, qgn, qgr, kgn, kgr, dqg, dkg, cosf, sinf):
    _, S, D = x.shape
    ts = min(PROJ_TS, S)
    row = lambda n: pl.BlockSpec((1, n), lambda b, s: (0, 0))
    full = lambda a: pl.BlockSpec(a.shape, lambda b, s: (0, 0))
    head_out = lambda w: pl.BlockSpec((1, MLA_HEADS, ts, w), lambda b, s: (b, 0, s, 0))
    return pl.pallas_call(
        _proj_kernel,
        grid=(B, S // ts),
        in_specs=[
            pl.BlockSpec((1, ts, D), lambda b, s: (b0 + b, s, 0)),
            pl.BlockSpec((1, 1, 1, D), lambda b, s: (b, 1, 0, 0)),
            pl.BlockSpec((1, 1, 1, D), lambda b, s: (b, 0, 0, 0)),
            row(D), full(win_p), row(MLA_Q_LORA), full(wq_p), row(MLA_KV_LORA), full(wkv),
            row(LANES), row(LANES), row(LANES), row(LANES), row(LANES), row(LANES),
            pl.BlockSpec((ts, LANES), lambda b, s: (s, 0)),
            pl.BlockSpec((ts, LANES), lambda b, s: (s, 0)),
        ],
        out_specs=[
            head_out(256), head_out(256), head_out(128),
            pl.BlockSpec((1, DIFF_HEADS, 2, ts, LANES), lambda b, s: (b, 0, 0, s, 0)),
            head_out(128), head_out(128),
        ],
        out_shape=[
            jax.ShapeDtypeStruct((B, MLA_HEADS, S, 256), BF16),
            jax.ShapeDtypeStruct((B, MLA_HEADS, S, 256), BF16),
            jax.ShapeDtypeStruct((B, MLA_HEADS, S, MLA_V), BF16),
            jax.ShapeDtypeStruct((B, DIFF_HEADS, 2, S, LANES), BF16),
            jax.ShapeDtypeStruct((B, DIFF_HEADS, S, LANES), BF16),
            jax.ShapeDtypeStruct((B, DIFF_HEADS, S, DIFF_DV), BF16),
        ],
        compiler_params=pltpu.CompilerParams(dimension_semantics=("arbitrary", "arbitrary"),
                                             vmem_limit_bytes=VMEM_LIMIT_BYTES),
        name="proj",
    )(x, mod4, mod4, n1g, win_p, qlg, wq_p, kvlg, wkv, qgn, qgr, kgn, kgr, dqg, dkg, cosf, sinf)


def _online_step(s, v, m, l, acc):
    m_new = jnp.maximum(m, jnp.max(s, axis=-1, keepdims=True))
    p = jnp.exp(s - m_new)
    alpha = jnp.exp(m - m_new)
    l = alpha * l + jnp.sum(p, axis=-1, keepdims=True)
    acc = alpha * acc + jnp.dot(p.astype(BF16), v, preferred_element_type=F32)
    return m_new, l, acc


_NT = (((1,), (1,)), ((), ()))


def _diag_mask(t):
    return lax.broadcasted_iota(jnp.int32, (t, t), 0) >= lax.broadcasted_iota(jnp.int32, (t, t), 1)


def _state(t, dv):
    return (jnp.full((t, 1), NEG_INF, F32), jnp.zeros((t, 1), F32), jnp.zeros((t, dv), F32))


def _mla_kernel(q_ref, k_ref, v_ref, o_ref, *, t):
    qi = pl.program_id(1)
    nh = q_ref.shape[1]
    qs = [q_ref[0, h] for h in range(nh)]
    mask = _diag_mask(t)

    def block(j, carry, diagonal):
        off = pl.multiple_of(j * t, t)
        ss = [lax.dot_general(qs[h], k_ref[0, h, pl.ds(off, t), :], _NT, preferred_element_type=F32)
              for h in range(nh)]
        if diagonal:
            ss = [jnp.where(mask, s, NEG_INF) for s in ss]
        return tuple(_online_step(ss[h], v_ref[0, h, pl.ds(off, t), :], *carry[h]) for h in range(nh))

    carry = lax.fori_loop(0, qi, lambda j, c: block(j, c, False), tuple(_state(t, MLA_V) for _ in range(nh)))
    carry = block(qi, carry, True)
    for h in range(nh):
        _, l, acc = carry[h]
        o_ref[0, :, h * MLA_V:(h + 1) * MLA_V] = (acc / l).astype(o_ref.dtype)


def _diff_kernel(lam_ref, g_ref, q_ref, k_ref, v_ref, o_ref, *, t, out_scale):
    qi = pl.program_id(1)
    nh = q_ref.shape[1]
    qs = [(q_ref[0, h, 0], q_ref[0, h, 1]) for h in range(nh)]
    mask = _diag_mask(t)

    def block(j, carry, diagonal):
        off = pl.multiple_of(j * t, t)
        out = []
        for hp in range(0, nh, 2):
            ss = [lax.dot_general(qs[h][p], k_ref[0, h, pl.ds(off, t), :], _NT, preferred_element_type=F32)
                  for h in (hp, hp + 1) for p in range(2)]
            if diagonal:
                ss = [jnp.where(mask, s, NEG_INF) for s in ss]
            for i, s in enumerate(ss):
                h = hp + i // 2
                out.append(_online_step(s, v_ref[0, h, pl.ds(off, t), :], *carry[2 * hp + i]))
        return tuple(out)

    carry = lax.fori_loop(0, qi, lambda j, c: block(j, c, False),
                          tuple(_state(t, DIFF_DV) for _ in range(2 * nh)))
    carry = block(qi, carry, True)
    for h in range(nh):
        _, l1, a1 = carry[2 * h]
        _, l2, a2 = carry[2 * h + 1]
        o = a1 / l1 - lam_ref[...] * (a2 / l2)
        o = _rms(o, DIFF_DV) * g_ref[...] * out_scale
        o_ref[0, :, h * DIFF_DV:(h + 1) * DIFF_DV] = o.astype(o_ref.dtype)


def _mla_attention(qm, km, vm):
    B, H, S, _ = qm.shape
    t = min(ATT_T, S)
    return pl.pallas_call(
        functools.partial(_mla_kernel, t=t),
        grid=(B, S // t),
        in_specs=[
            pl.BlockSpec((1, H, t, 256), lambda b, i: (b, 0, i, 0)),
            pl.BlockSpec((1, H, S, 256), lambda b, i: (b, 0, 0, 0)),
            pl.BlockSpec((1, H, S, MLA_V), lambda b, i: (b, 0, 0, 0)),
        ],
        out_specs=pl.BlockSpec((1, t, H * MLA_V), lambda b, i: (b, i, 0)),
        out_shape=jax.ShapeDtypeStruct((B, S, H * MLA_V), BF16),
        compiler_params=pltpu.CompilerParams(dimension_semantics=("arbitrary",) * 2,
                                             vmem_limit_bytes=VMEM_LIMIT_BYTES),
        name="mla_attn",
    )(qm, km, vm)


def _diff_attention(lam, subln_g, qd, kd, vd, out_scale):
    B, H, _, S, _ = qd.shape
    t = min(ATT_T, S)
    return pl.pallas_call(
        functools.partial(_diff_kernel, t=t, out_scale=out_scale),
        grid=(B, S // t),
        in_specs=[
            pl.BlockSpec((1, LANES), lambda b, i: (0, 0)),
            pl.BlockSpec((1, DIFF_DV), lambda b, i: (0, 0)),
            pl.BlockSpec((1, H, 2, t, LANES), lambda b, i: (b, 0, 0, i, 0)),
            pl.BlockSpec((1, H, S, LANES), lambda b, i: (b, 0, 0, 0)),
            pl.BlockSpec((1, H, S, DIFF_DV), lambda b, i: (b, 0, 0, 0)),
        ],
        out_specs=pl.BlockSpec((1, t, H * DIFF_DV), lambda b, i: (b, i, 0)),
        out_shape=jax.ShapeDtypeStruct((B, S, H * DIFF_DV), BF16),
        compiler_params=pltpu.CompilerParams(dimension_semantics=("arbitrary",) * 2,
                                             vmem_limit_bytes=VMEM_LIMIT_BYTES),
        name="diff_attn",
    )(lam, subln_g, qd, kd, vd)


def _topk_rows(vals, k, extra=None, order=None):
    row = lax.broadcasted_iota(jnp.int32, vals.shape, 0) if order is None else order
    n = jnp.iinfo(jnp.int32).max
    out_v, out_i, out_e = [], [], []
    for _ in range(k):
        m = jnp.max(vals, axis=0, keepdims=True)
        sel = jnp.min(jnp.where(vals == m, row, n), axis=0, keepdims=True)
        hit = row == sel
        out_v.append(m)
        out_i.append(sel)
        if extra is not None:
            out_e.append(jnp.max(jnp.where(hit, extra, -1), axis=0, keepdims=True))
        vals = jnp.where(hit, NEG_INF, vals)
    cat = lambda xs: jnp.concatenate(xs, axis=0)
    return cat(out_v), cat(out_i), (cat(out_e) if extra is not None else None)


def _route_kernel(x_ref, mm_ref, md_ref, woa_ref, wob_ref, g1_ref, sc_ref, sh_ref, n2g_ref, wq_ref, keys_ref,
                  x1_ref, h2_ref, eidx_ref, gt_ref, st_scr, sv_scr, si_scr, gt_scr, et_scr):
    x = x_ref[0]
    D = x.shape[-1]
    o = jnp.dot(mm_ref[0], woa_ref[...], preferred_element_type=F32)
    o = o + jnp.dot(md_ref[0], wob_ref[...], preferred_element_type=F32)
    x1 = x + g1_ref[0, 0] * o
    x1_ref[0] = x1
    h2 = _rms(x1, D) * n2g_ref[...] * (1.0 + sc_ref[0, 0]) + sh_ref[0, 0]
    h2_ref[0] = h2
    q = jnp.dot(h2.astype(BF16), wq_ref[...], preferred_element_type=F32).astype(BF16)
    ngroups = 2 * PEER_HEADS
    for g in range(ngroups):
        st_scr[g] = lax.dot_general(keys_ref[g], q[:, g * PEER_DK_HALF:(g + 1) * PEER_DK_HALF], _NT,
                                    preferred_element_type=F32)

    def sub_topk(g2, carry):
        for g in (2 * g2, 2 * g2 + 1):
            v, i, _ = _topk_rows(st_scr[g], PEER_TOPK)
            sv_scr[g] = v
            si_scr[g] = i
        return carry

    lax.fori_loop(0, ngroups // 2, sub_topk, 0)

    def head_topk(hd, carry):
        v0, v1 = sv_scr[2 * hd], sv_scr[2 * hd + 1]
        i0, i1 = si_scr[2 * hd], si_scr[2 * hd + 1]
        k = PEER_TOPK
        ts = v0.shape[1]
        sub = lax.broadcasted_iota(jnp.int32, (8, ts), 0)

        def piece(parts):
            val = jnp.full((8, ts), NEG_INF, F32)
            idx = jnp.zeros((8, ts), jnp.int32)
            pos = jnp.full((8, ts), k * k, jnp.int32)
            for a, r0, cnt in parts:
                v1s = v1[0:8] if r0 == 0 else pltpu.roll(v1[0:8], r0, 0)
                i1s = i1[0:8] if r0 == 0 else pltpu.roll(i1[0:8], r0, 0)
                inside = (sub >= r0) & (sub < r0 + cnt)
                val = jnp.where(inside, v0[a:a + 1] + v1s, val)
                idx = jnp.where(inside, i0[a:a + 1] * PEER_NKEYS + i1s, idx)
                pos = jnp.where(inside, a * k + sub - r0, pos)
            return val, idx, pos

        tiles = [(v0[0:1] + v1[0:8], i0[0:1] * PEER_NKEYS + i1[0:8], sub),
                 (v0[0:1] + v1[8:k], i0[0:1] * PEER_NKEYS + i1[8:k], sub + 8),
                 piece([(1, 0, 8)]), piece([(2, 0, 5), (4, 5, 3)]), piece([(3, 0, 4), (5, 4, 2), (6, 6, 2)]),
                 piece([(7, 0, 2)]),
                 (v0[8:k] + v1[0:1], i0[8:k] * PEER_NKEYS + i1[0:1], (sub + 8) * k)]
        cat = lambda j: jnp.concatenate([t[j] for t in tiles], axis=0)
        fv, _, e = _topk_rows(cat(0), k, extra=cat(1), order=cat(2))
        p = jnp.exp(fv - fv[0:1])
        gate = p / jnp.sum(p, axis=0, keepdims=True)
        r0 = pl.multiple_of(hd * PEER_TOPK, PEER_TOPK)
        et_scr[pl.ds(r0, PEER_TOPK), :] = e
        gt_scr[pl.ds(r0, PEER_TOPK), :] = gate
        return carry

    lax.fori_loop(0, PEER_HEADS, head_topk, 0)
    gt_ref[...] = gt_scr[...]
    eidx_ref[...] = et_scr[...].T


def _route(x, b0, B, mixed_m, mixed_d, woa, wob, mod4, n2g, wq, keys):
    _, S, D = x.shape
    T = B * S
    ts = min(ROUTE_TS, S)
    nst = S // ts
    full = lambda a: pl.BlockSpec(a.shape, lambda b, s: (0,) * a.ndim)
    modspec = lambda k: pl.BlockSpec((1, 1, 1, D), lambda b, s: (b, k, 0, 0))
    tok = lambda w: pl.BlockSpec((1, ts, w), lambda b, s: (b, s, 0))
    return pl.pallas_call(
        _route_kernel,
        grid=(B, nst),
        in_specs=[
            pl.BlockSpec((1, ts, D), lambda b, s: (b0 + b, s, 0)),
            tok(mixed_m.shape[-1]), tok(mixed_d.shape[-1]), full(woa), full(wob),
            modspec(2), modspec(4), modspec(3),
            pl.BlockSpec((1, D), lambda b, s: (0, 0)), full(wq), full(keys),
        ],
        out_specs=[
            tok(D), tok(D),
            pl.BlockSpec((ts, PEER_NE), lambda b, s: (b * nst + s, 0)),
            pl.BlockSpec((PEER_NE, ts), lambda b, s: (0, b * nst + s)),
        ],
        out_shape=[
            jax.ShapeDtypeStruct((B, S, D), F32),
            jax.ShapeDtypeStruct((B, S, D), F32),
            jax.ShapeDtypeStruct((T, PEER_NE), jnp.int32),
            jax.ShapeDtypeStruct((PEER_NE, T), F32),
        ],
        scratch_shapes=[
            pltpu.VMEM((2 * PEER_HEADS, PEER_NKEYS, ts), F32),
            pltpu.VMEM((2 * PEER_HEADS, PEER_TOPK, ts), F32),
            pltpu.VMEM((2 * PEER_HEADS, PEER_TOPK, ts), jnp.int32),
            pltpu.VMEM((PEER_NE, ts), F32),
            pltpu.VMEM((PEER_NE, ts), jnp.int32),
        ],
        compiler_params=pltpu.CompilerParams(dimension_semantics=("arbitrary", "arbitrary"),
                                             vmem_limit_bytes=VMEM_LIMIT_BYTES),
        name="route",
    )(x, mixed_m, mixed_d, woa, wob, mod4, mod4, mod4, n2g, wq, keys)


_SQRT_HALF = float(np.sqrt(0.5))


def _gelu(a):
    return 0.5 * a * (1.0 + lax.erf(a * _SQRT_HALF))


def _mix_token(tile, hrow, gate, before_a=None, before_y=None):
    nchunk = hrow.shape[-1] // LANES
    ngrp = gate.shape[0] // 8
    acts = []
    for g in range(ngrp):
        if before_a is not None:
            before_a(g)
        acc = jnp.zeros((8, LANES), F32)
        for c in range(nchunk):
            u = lax.bitcast_convert_type(tile(g, c) << 16, F32)
            acc = acc + u * hrow[:, c * LANES:(c + 1) * LANES]
        a = jnp.sum(acc, axis=1, keepdims=True)
        acts.append(_gelu(a) * gate[g * 8:(g + 1) * 8])
    yacc = [jnp.zeros((8, LANES), F32) for _ in range(nchunk)]
    for g in range(ngrp):
        if before_y is not None:
            before_y(g)
        for c in range(nchunk):
            v = lax.bitcast_convert_type(tile(g, c) & jnp.uint32(0xFFFF0000), F32)
            yacc[c] = yacc[c] + v * acts[g]
    return jnp.concatenate([jnp.sum(yc, axis=0, keepdims=True) for yc in yacc], axis=1)


def _gate_column(gt_ref, token):
    lane = lax.broadcasted_iota(jnp.int32, gt_ref.shape, 1)
    return jnp.sum(jnp.where(lane == token, gt_ref[...], 0.0), axis=1, keepdims=True)


def _peer_kernel(idx_ref, idxn_ref, h_ref, gt_ref, x1_ref, g2_ref, tab_ref, o_ref, buf, sem, *, tt):
    ne = PEER_NE
    rt = PEER_ROW_TILE
    look = PEER_NSLOT - 1
    nchunk = h_ref.shape[-1] // LANES
    ngrp = ne // 8
    step = pl.program_id(0)
    nsteps = pl.num_programs(0)
    nparts = 2 * ngrp
    per = ne // nparts

    def issue_part(iref, t, slot, part):
        for e in range(part * per, (part + 1) * per):
            row = pl.multiple_of(iref[t, e] * rt, rt)
            g, r = divmod(e, 8)
            pltpu.make_async_copy(tab_ref.at[pl.ds(row, rt), :], buf.at[slot, pl.ds(g * nchunk, nchunk), r, :],
                                  sem.at[slot]).start(priority=e % 2)

    def wait(slot):
        pltpu.make_async_copy(buf.at[slot], buf.at[slot], sem.at[slot]).wait()

    @pl.when(step == 0)
    def _():
        for t in range(look):
            for part in range(nparts):
                issue_part(idx_ref, t, t, part)

    def group(base, nxt_of):
        h8 = h_ref[pl.ds(base, 8), :]
        ys = []
        for j in range(8):
            iref, tn = nxt_of(j)
            nslot = (j + look) % PEER_NSLOT
            slot = j % PEER_NSLOT
            wait(slot)
            ys.append(_mix_token(lambda g, c: buf[slot, g * nchunk + c], h8[j:j + 1], _gate_column(gt_ref, base + j),
                                 before_a=lambda g: issue_part(iref, tn, nslot, g),
                                 before_y=lambda g: issue_part(iref, tn, nslot, ngrp + g)))
        y = jnp.concatenate(ys, axis=0)
        o_ref[pl.ds(base, 8), :] = x1_ref[pl.ds(base, 8), :] + g2_ref[0, 0] * y

    def body(gi, carry):
        base = pl.multiple_of(gi * 8, 8)
        group(base, lambda j: (idx_ref, base + j + look))
        return carry

    lax.fori_loop(0, tt // 8 - 1, body, 0)
    last = tt - 8
    group(last, lambda j: (idx_ref, last + j + look) if j + look < 8 else (idxn_ref, j + look - 8))

    @pl.when(step == nsteps - 1)
    def _():
        for t in range(look):
            wait(t)


def _peer_gathered(eidx, h2, gt, x1, mod4, table, nblk, S):
    T, D = h2.shape
    tt = min(PEER_TT, S)
    nst = S // tt
    nchunk = D // LANES
    return pl.pallas_call(
        functools.partial(_peer_kernel, tt=tt),
        grid=(nblk,),
        in_specs=[
            pl.BlockSpec((tt, PEER_NE), lambda i: (i, 0), memory_space=pltpu.SMEM),
            pl.BlockSpec((tt, PEER_NE), lambda i: (jnp.minimum(i + 1, nblk - 1), 0), memory_space=pltpu.SMEM),
            pl.BlockSpec((tt, D), lambda i: (i, 0)),
            pl.BlockSpec((PEER_NE, tt), lambda i: (0, i)),
            pl.BlockSpec((tt, D), lambda i: (i, 0)),
            pl.BlockSpec((1, 1, 1, D), lambda i: (i // nst, 5, 0, 0)),
            pl.BlockSpec(memory_space=pl.ANY),
        ],
        out_specs=pl.BlockSpec((tt, D), lambda i: (i, 0)),
        out_shape=jax.ShapeDtypeStruct((T, D), F32),
        scratch_shapes=[pltpu.VMEM((PEER_NSLOT, PEER_NE // 8 * nchunk, 8, LANES), jnp.uint32),
                        pltpu.SemaphoreType.DMA((PEER_NSLOT,))],
        compiler_params=pltpu.CompilerParams(dimension_semantics=("arbitrary",),
                                             vmem_limit_bytes=VMEM_LIMIT_BYTES),
        name="peer_gather",
    )(eidx, eidx, h2, gt, x1, mod4, table)


def _sc_gather(table, idx):
    nb = idx.shape[0]
    d = table.shape[1]
    per_w = nb // SC_WORKERS
    nchunks = per_w // SC_CHUNK
    mesh = plsc.VectorSubcoreMesh(core_axis_name="c", subcore_axis_name="s")

    @functools.partial(
        pl.kernel, mesh=mesh,
        out_type=jax.ShapeDtypeStruct((nb, d), table.dtype),
        scratch_types=[pltpu.VMEM((SC_CHUNK,), jnp.int32), pltpu.VMEM((SC_CHUNK, d), table.dtype),
                       pltpu.SemaphoreType.DMA],
    )
    def k(table_hbm, idx_hbm, out_hbm, idx_v, rows_v, sem):
        wid = lax.axis_index("s") * 2 + lax.axis_index("c")
        base = wid * per_w

        @pl.loop(0, nchunks)
        def _(i):
            off = pl.multiple_of(base + i * SC_CHUNK, SC_CHUNK)
            pltpu.sync_copy(idx_hbm.at[pl.ds(off, SC_CHUNK)], idx_v)
            pltpu.async_copy(table_hbm.at[idx_v], rows_v, sem).wait()
            pltpu.sync_copy(rows_v, out_hbm.at[pl.ds(off, SC_CHUNK)])

    return k(table, idx)


def _dense_kernel(rows_ref, h_ref, gt_ref, x1_ref, g2_ref, prev_ref, o_ref):
    ne = PEER_NE
    nd = h_ref.shape[0]
    i = pl.program_id(0)
    per_blk = gt_ref.shape[1] // nd
    ys = []
    for j in range(nd):
        tile = lambda g, c, j=j: rows_ref[j * ne + g * 8:j * ne + (g + 1) * 8, c * LANES:(c + 1) * LANES]
        ys.append(_mix_token(tile, h_ref[j:j + 1, :], _gate_column(gt_ref, (i % per_blk) * nd + j)))
    o_ref[...] = x1_ref[...] + g2_ref[0, 0] * jnp.concatenate(ys, axis=0)


def _peer_dense(rows, h2, gt, x1, mod4, out, blk0, nblk, S):
    T, D = h2.shape
    tt = min(PEER_TT, S)
    nst = S // tt
    nd = DENSE_TOK
    per_blk = tt // nd
    return pl.pallas_call(
        _dense_kernel,
        grid=(nblk * per_blk,),
        in_specs=[
            pl.BlockSpec((nd * PEER_NE, D), lambda i: (i, 0)),
            pl.BlockSpec((nd, D), lambda i: (blk0 * per_blk + i, 0)),
            pl.BlockSpec((PEER_NE, tt), lambda i: (0, blk0 + i // per_blk)),
            pl.BlockSpec((nd, D), lambda i: (blk0 * per_blk + i, 0)),
            pl.BlockSpec((1, 1, 1, D), lambda i: ((blk0 + i // per_blk) // nst, 5, 0, 0)),
            pl.BlockSpec(memory_space=pl.ANY),
        ],
        out_specs=pl.BlockSpec((nd, D), lambda i: (blk0 * per_blk + i, 0)),
        out_shape=jax.ShapeDtypeStruct((T, D), F32),
        input_output_aliases={5: 0},
        compiler_params=pltpu.CompilerParams(dimension_semantics=("arbitrary",),
                                             vmem_limit_bytes=VMEM_LIMIT_BYTES),
        name="peer_dense",
    )(rows, h2, gt, x1, mod4, out)


def _rope_tables(S):
    half = MLA_ROPE // 2
    inv = 1.0 / (ROPE_THETA ** (jnp.arange(half, dtype=F32) / half))
    ang = jnp.arange(S, dtype=F32)[:, None] * inv[None, :]
    cos, sin = jnp.cos(ang), jnp.sin(ang)
    cosf = jnp.tile(cos, (1, 4))
    sinf = jnp.concatenate([-sin, sin, -sin, sin], axis=1)
    return cosf, sinf


def _pad_lanes(v, fill):
    return jnp.concatenate([v, jnp.full((LANES - v.shape[0],), fill, v.dtype)]).reshape(1, LANES)


def _pack_table(u, v):
    ub = lax.bitcast_convert_type(u.astype(BF16), jnp.uint16).astype(jnp.uint32)
    vb = lax.bitcast_convert_type(v.astype(BF16), jnp.uint16).astype(jnp.uint32)
    return ub | (vb << 16)


def kernel(x, c, ada_w, ada_b, norm1_g, w_in, mla_q_lat_g, mla_w_q_up, mla_kv_lat_g, mla_w_kv_up, mla_q_g, mla_k_g, diff_q_g, diff_k_g, diff_lq1, diff_lk1, diff_lq2, diff_lk2, diff_subln_g, w_out, norm2_g, peer_w_q, peer_sub_keys, peer_u, peer_v):
    B, S, D = x.shape
    depth = ada_w.shape[0]
    cosf, sinf = _rope_tables(S)
    o0 = MLA_Q_LORA
    o1 = o0 + MLA_KV_LORA
    o2 = o1 + MLA_ROPE
    nq = DIFF_HEADS * 2 * DIFF_DQK
    o3 = o2 + nq
    o4 = o3 + nq
    for l in range(depth):
        lambda_init = 0.8 - 0.6 * math.exp(-0.3 * l)
        mod, lam = _ada(c, ada_w[l], ada_b[l], diff_lq1[l], diff_lk1[l], diff_lq2[l], diff_lk2[l], lambda_init)
        mod4 = mod.reshape(B, 6, 1, D)

        wi = w_in[l]
        win_p = jnp.concatenate([wi[:, :o1], wi[:, o2:], wi[:, o1:o2], jnp.zeros((D, LANES - MLA_ROPE), F32)],
                                axis=1).astype(BF16)
        wq3 = mla_w_q_up[l].reshape(MLA_Q_LORA, MLA_HEADS, MLA_QK)
        wq_p = jnp.concatenate([wq3, jnp.zeros((MLA_Q_LORA, MLA_HEADS, 256 - MLA_QK), F32)], axis=2)
        wq_p = wq_p.reshape(MLA_Q_LORA, MLA_HEADS * 256).astype(BF16)
        wkv = mla_w_kv_up[l].astype(BF16)
        wo = w_out[l].astype(BF16)
        nm = MLA_HEADS * MLA_V
        keys = peer_sub_keys[l].reshape(2 * PEER_HEADS, PEER_NKEYS, PEER_DK_HALF).astype(BF16)
        wpq = peer_w_q[l].astype(BF16)
        table = _pack_table(peer_u[l], peer_v[l])
        tab_tc = table.reshape(-1, LANES)

        def routed(b0, nb, m4):
            qm, km, vm, qd, kd, vd = _proj(
                x, b0, nb, m4, norm1_g[l].reshape(1, D), win_p, mla_q_lat_g[l].reshape(1, -1), wq_p,
                mla_kv_lat_g[l].reshape(1, -1), wkv,
                mla_q_g[l, :MLA_NOPE].reshape(1, -1), _pad_lanes(mla_q_g[l, MLA_NOPE:], 1.0),
                mla_k_g[l, :MLA_NOPE].reshape(1, -1), _pad_lanes(mla_k_g[l, MLA_NOPE:], 1.0),
                jnp.tile(diff_q_g[l], 2).reshape(1, -1), jnp.tile(diff_k_g[l], 2).reshape(1, -1), cosf, sinf)
            mixed_m = _mla_attention(qm, km, vm)
            mixed_d = _diff_attention(lam, diff_subln_g[l].reshape(1, -1), qd, kd, vd, 1.0 - lambda_init)
            return _route(x, b0, nb, mixed_m, mixed_d, wo[:nm], wo[nm:], m4, norm2_g[l].reshape(1, D), wpq, keys)

        groups = SEQ_GROUPS if sum(g[0] for g in SEQ_GROUPS) == B else ((B,) + SEQ_GROUPS[0][1:],)
        tt = min(PEER_TT, S)
        starts = [sum(g[0] for g in groups[:i]) for i in range(len(groups))]
        m4s = [mod4[b0:b0 + g[0]] for b0, g in zip(starts, groups)]
        rts = [routed(b0, g[0], m4) for b0, g, m4 in zip(starts, groups, m4s)]
        outs = []
        for (nb, num, den), m4, (x1, h2, eidx, gt) in zip(groups, m4s, rts):
            nblk = nb * S // tt
            nsc = (nblk * num // den) if nblk >= den else 0
            ntc = nblk - nsc
            h2f, x1f = h2.reshape(nb * S, D), x1.reshape(nb * S, D)
            rows = _sc_gather(table, eidx[ntc * tt:].reshape(-1)) if nsc else None
            out = _peer_gathered(eidx, h2f, gt, x1f, m4, tab_tc, ntc, S)
            if nsc:
                out = _peer_dense(rows, h2f, gt, x1f, m4, out, ntc, nsc, S)
            outs.append(out.reshape(nb, S, D))
        x = jnp.concatenate(outs, axis=0) if len(outs) > 1 else outs[0]
    return x
```

```python
import functools
import math

import jax
import jax.numpy as jnp
import numpy as np
from jax import lax
from jax.experimental import pallas as pl
from jax.experimental.pallas import tpu as pltpu
from jax.experimental.pallas import tpu_sc as plsc

F32 = jnp.float32
BF16 = jnp.bfloat16

MLA_HEADS = 4
MLA_NOPE = 128
MLA_ROPE = 64
MLA_QK = MLA_NOPE + MLA_ROPE
MLA_V = 128
MLA_Q_LORA = 384
MLA_KV_LORA = 256
DIFF_HEADS = 4
DIFF_DQK = 64
DIFF_DV = 128
ROPE_THETA = 10000.0
RMS_EPS = 1e-6
PEER_HEADS = 8
PEER_NKEYS = 128
PEER_DK_HALF = 128
PEER_TOPK = 16
NEG_INF = float("-inf")

LANES = 128
VMEM_LIMIT_BYTES = 56 * 1024 * 1024

PROJ_TS = 512
ATT_T = 512
ROUTE_TS = 256
PEER_TT = 128
PEER_NSLOT = 8
PEER_NE = PEER_HEADS * PEER_TOPK
PEER_ROW_TILE = 8
SC_WORKERS = 32
SC_CHUNK = 64
DENSE_TOK = 32
SEQ_GROUPS = ((2, 3, 4), (4, 11, 16), (5, 5, 8), (5, 9, 16))


def _rms(x, n):
    ss = jnp.sum(x * x, axis=-1, keepdims=True)
    return x * lax.rsqrt(ss * (1.0 / n) + RMS_EPS)


def _swap_halves64(x):
    lane = lax.broadcasted_iota(jnp.int32, x.shape, x.ndim - 1)
    first = (lane % 64) < 32
    return jnp.where(first, pltpu.roll(x, 96, x.ndim - 1), pltpu.roll(x, 32, x.ndim - 1))


def _rope(x, cosf, sinf):
    return x * cosf + _swap_halves64(x) * sinf


def _ada_kernel(c_ref, w_ref, b_ref, lq1_ref, lk1_ref, lq2_ref, lk2_ref, mod_ref, lam_ref, *, lambda_init):
    c = c_ref[...]
    s = c * jax.nn.sigmoid(c)
    mod_ref[...] = jnp.dot(s, w_ref[...], preferred_element_type=F32) + b_ref[...]
    d1 = jnp.sum(lq1_ref[...] * lk1_ref[...], axis=-1, keepdims=True)
    d2 = jnp.sum(lq2_ref[...] * lk2_ref[...], axis=-1, keepdims=True)
    lam = jnp.exp(d1) - jnp.exp(d2) + lambda_init
    lam_ref[...] = jnp.broadcast_to(lam, lam_ref.shape)


def _ada(c, ada_w, ada_b, lq1, lk1, lq2, lk2, lambda_init):
    B, D = c.shape
    N = ada_w.shape[1]
    bn = 1024
    small = pl.BlockSpec((1, DIFF_DQK), lambda j: (0, 0))
    return pl.pallas_call(
        functools.partial(_ada_kernel, lambda_init=lambda_init),
        grid=(N // bn,),
        in_specs=[
            pl.BlockSpec((B, D), lambda j: (0, 0)),
            pl.BlockSpec((D, bn), lambda j: (0, j)),
            pl.BlockSpec((1, bn), lambda j: (0, j)),
            small, small, small, small,
        ],
        out_specs=[pl.BlockSpec((B, bn), lambda j: (0, j)), pl.BlockSpec((1, LANES), lambda j: (0, 0))],
        out_shape=[jax.ShapeDtypeStruct((B, N), F32), jax.ShapeDtypeStruct((1, LANES), F32)],
        compiler_params=pltpu.CompilerParams(dimension_semantics=("arbitrary",)),
        name="ada",
    )(c, ada_w, ada_b.reshape(1, N), lq1.reshape(1, -1), lk1.reshape(1, -1), lq2.reshape(1, -1), lk2.reshape(1, -1))


C_QLAT = 0
C_KVLAT = C_QLAT + MLA_Q_LORA
C_DQ = C_KVLAT + MLA_KV_LORA
C_DK = C_DQ + DIFF_HEADS * 2 * DIFF_DQK
C_DV = C_DK + DIFF_HEADS * 2 * DIFF_DQK
C_KPE = C_DV + DIFF_HEADS * DIFF_DV
C_END = C_KPE + LANES


def _proj_kernel(x_ref, sc_ref, sh_ref, n1g_ref, win_ref, qlg_ref, wq_ref, kvlg_ref, wkv_ref,
                 qgn_ref, qgr_ref, kgn_ref, kgr_ref, dqg_ref, dkg_ref, cos_ref, sin_ref,
                 qm_ref, km_ref, vm_ref, qd_ref, kd_ref, vd_ref):
    x = x_ref[0]
    D = x.shape[-1]
    h = _rms(x, D) * n1g_ref[...] * (1.0 + sc_ref[0, 0]) + sh_ref[0, 0]
    proj = jnp.dot(h.astype(BF16), win_ref[...], preferred_element_type=F32)
    cosf = cos_ref[...]
    sinf = sin_ref[...]

    q_lat = proj[:, C_QLAT:C_QLAT + MLA_Q_LORA]
    q = jnp.dot((_rms(q_lat, MLA_Q_LORA) * qlg_ref[...]).astype(BF16), wq_ref[...], preferred_element_type=F32)
    kv_lat = proj[:, C_KVLAT:C_KVLAT + MLA_KV_LORA]
    kv = jnp.dot((_rms(kv_lat, MLA_KV_LORA) * kvlg_ref[...]).astype(BF16), wkv_ref[...], preferred_element_type=F32)
    kpe = _rope(_rms(proj[:, C_KPE:C_KPE + LANES], MLA_ROPE) * kgr_ref[...], cosf, sinf).astype(BF16)
    q_scale = MLA_QK ** -0.5
    for hd in range(MLA_HEADS):
        qn = _rms(q[:, hd * 256:hd * 256 + 128], MLA_NOPE) * qgn_ref[...]
        qr = _rope(_rms(q[:, hd * 256 + 128:hd * 256 + 256], MLA_ROPE) * qgr_ref[...], cosf, sinf)
        qm_ref[0, hd, :, 0:128] = (qn * q_scale).astype(BF16)
        qm_ref[0, hd, :, 128:256] = (qr * q_scale).astype(BF16)
        kn = _rms(kv[:, hd * 256:hd * 256 + 128], MLA_NOPE) * kgn_ref[...]
        km_ref[0, hd, :, 0:128] = kn.astype(BF16)
        km_ref[0, hd, :, 128:256] = kpe
        vm_ref[0, hd] = kv[:, hd * 256 + 128:hd * 256 + 256].astype(BF16)

    lane = lax.broadcasted_iota(jnp.int32, (x.shape[0], LANES), 1)
    first = lane < DIFF_DQK

    def seg_norm(t):
        t2 = t * t
        s1 = jnp.sum(jnp.where(first, t2, 0.0), axis=-1, keepdims=True)
        s2 = jnp.sum(jnp.where(first, 0.0, t2), axis=-1, keepdims=True)
        r = jnp.where(first, lax.rsqrt(s1 * (1.0 / DIFF_DQK) + RMS_EPS), lax.rsqrt(s2 * (1.0 / DIFF_DQK) + RMS_EPS))
        return t * r

    d_scale = DIFF_DQK ** -0.5
    for hd in range(DIFF_HEADS):
        tq = _rope(seg_norm(proj[:, C_DQ + hd * 128:C_DQ + (hd + 1) * 128]) * dqg_ref[...], cosf, sinf) * d_scale
        qd_ref[0, hd, 0] = jnp.where(first, tq, 0.0).astype(BF16)
        qd_ref[0, hd, 1] = jnp.where(first, 0.0, tq).astype(BF16)
        tk = _rope(seg_norm(proj[:, C_DK + hd * 128:C_DK + (hd + 1) * 128]) * dkg_ref[...], cosf, sinf)
        kd_ref[0, hd] = tk.astype(BF16)
        vd_ref[0, hd] = proj[:, C_DV + hd * 128:C_DV + (hd + 1) * 128].astype(BF16)


def _proj(x, b0, B, mod4, n1g, win_p, qlg, wq_p, kvlg, wkv, qgn, qgr, kgn, kgr, dqg, dkg, cosf, sinf):
    _, S, D = x.shape
    ts = min(PROJ_TS, S)
    row = lambda n: pl.BlockSpec((1, n), lambda b, s: (0, 0))
    full = lambda a: pl.BlockSpec(a.shape, lambda b, s: (0, 0))
    head_out = lambda w: pl.BlockSpec((1, MLA_HEADS, ts, w), lambda b, s: (b, 0, s, 0))
    return pl.pallas_call(
        _proj_kernel,
        grid=(B, S // ts),
        in_specs=[
            pl.BlockSpec((1, ts, D), lambda b, s: (b0 + b, s, 0)),
            pl.BlockSpec((1, 1, 1, D), lambda b, s: (b, 1, 0, 0)),
            pl.BlockSpec((1, 1, 1, D), lambda b, s: (b, 0, 0, 0)),
            row(D), full(win_p), row(MLA_Q_LORA), full(wq_p), row(MLA_KV_LORA), full(wkv),
            row(LANES), row(LANES), row(LANES), row(LANES), row(LANES), row(LANES),
            pl.BlockSpec((ts, LANES), lambda b, s: (s, 0)),
            pl.BlockSpec((ts, LANES), lambda b, s: (s, 0)),
        ],
        out_specs=[
            head_out(256), head_out(256), head_out(128),
            pl.BlockSpec((1, DIFF_HEADS, 2, ts, LANES), lambda b, s: (b, 0, 0, s, 0)),
            head_out(128), head_out(128),
        ],
        out_shape=[
            jax.ShapeDtypeStruct((B, MLA_HEADS, S, 256), BF16),
            jax.ShapeDtypeStruct((B, MLA_HEADS, S, 256), BF16),
            jax.ShapeDtypeStruct((B, MLA_HEADS, S, MLA_V), BF16),
            jax.ShapeDtypeStruct((B, DIFF_HEADS, 2, S, LANES), BF16),
            jax.ShapeDtypeStruct((B, DIFF_HEADS, S, LANES), BF16),
            jax.ShapeDtypeStruct((B, DIFF_HEADS, S, DIFF_DV), BF16),
        ],
        compiler_params=pltpu.CompilerParams(dimension_semantics=("arbitrary", "arbitrary"),
                                             vmem_limit_bytes=VMEM_LIMIT_BYTES),
        name="proj",
    )(x, mod4, mod4, n1g, win_p, qlg, wq_p, kvlg, wkv, qgn, qgr, kgn, kgr, dqg, dkg, cosf, sinf)


def _online_step(s, v, m, l, acc):
    m_new = jnp.maximum(m, jnp.max(s, axis=-1, keepdims=True))
    p = jnp.exp(s - m_new)
    alpha = jnp.exp(m - m_new)
    l = alpha * l + jnp.sum(p, axis=-1, keepdims=True)
    acc = alpha * acc + jnp.dot(p.astype(BF16), v, preferred_element_type=F32)
    return m_new, l, acc


_NT = (((1,), (1,)), ((), ()))


def _diag_mask(t):
    return lax.broadcasted_iota(jnp.int32, (t, t), 0) >= lax.broadcasted_iota(jnp.int32, (t, t), 1)


def _state(t, dv):
    return (jnp.full((t, 1), NEG_INF, F32), jnp.zeros((t, 1), F32), jnp.zeros((t, dv), F32))


def _mla_kernel(q_ref, k_ref, v_ref, o_ref, *, t):
    qi = pl.program_id(1)
    nh = q_ref.shape[1]
    qs = [q_ref[0, h] for h in range(nh)]
    mask = _diag_mask(t)

    def block(j, carry, diagonal):
        off = pl.multiple_of(j * t, t)
        ss = [lax.dot_general(qs[h], k_ref[0, h, pl.ds(off, t), :], _NT, preferred_element_type=F32)
              for h in range(nh)]
        if diagonal:
            ss = [jnp.where(mask, s, NEG_INF) for s in ss]
        return tuple(_online_step(ss[h], v_ref[0, h, pl.ds(off, t), :], *carry[h]) for h in range(nh))

    carry = lax.fori_loop(0, qi, lambda j, c: block(j, c, False), tuple(_state(t, MLA_V) for _ in range(nh)))
    carry = block(qi, carry, True)
    for h in range(nh):
        _, l, acc = carry[h]
        o_ref[0, :, h * MLA_V:(h + 1) * MLA_V] = (acc / l).astype(o_ref.dtype)


def _diff_kernel(lam_ref, g_ref, q_ref, k_ref, v_ref, o_ref, *, t, out_scale):
    qi = pl.program_id(1)
    nh = q_ref.shape[1]
    qs = [(q_ref[0, h, 0], q_ref[0, h, 1]) for h in range(nh)]
    mask = _diag_mask(t)

    def block(j, carry, diagonal):
        off = pl.multiple_of(j * t, t)
        out = []
        for hp in range(0, nh, 2):
            ss = [lax.dot_general(qs[h][p], k_ref[0, h, pl.ds(off, t), :], _NT, preferred_element_type=F32)
                  for h in (hp, hp + 1) for p in range(2)]
            if diagonal:
                ss = [jnp.where(mask, s, NEG_INF) for s in ss]
            for i, s in enumerate(ss):
                h = hp + i // 2
                out.append(_online_step(s, v_ref[0, h, pl.ds(off, t), :], *carry[2 * hp + i]))
        return tuple(out)

    carry = lax.fori_loop(0, qi, lambda j, c: block(j, c, False),
                          tuple(_state(t, DIFF_DV) for _ in range(2 * nh)))
    carry = block(qi, carry, True)
    for h in range(nh):
        _, l1, a1 = carry[2 * h]
        _, l2, a2 = carry[2 * h + 1]
        o = a1 / l1 - lam_ref[...] * (a2 / l2)
        o = _rms(o, DIFF_DV) * g_ref[...] * out_scale
        o_ref[0, :, h * DIFF_DV:(h + 1) * DIFF_DV] = o.astype(o_ref.dtype)


def _mla_attention(qm, km, vm):
    B, H, S, _ = qm.shape
    t = min(ATT_T, S)
    return pl.pallas_call(
        functools.partial(_mla_kernel, t=t),
        grid=(B, S // t),
        in_specs=[
            pl.BlockSpec((1, H, t, 256), lambda b, i: (b, 0, i, 0)),
            pl.BlockSpec((1, H, S, 256), lambda b, i: (b, 0, 0, 0)),
            pl.BlockSpec((1, H, S, MLA_V), lambda b, i: (b, 0, 0, 0)),
        ],
        out_specs=pl.BlockSpec((1, t, H * MLA_V), lambda b, i: (b, i, 0)),
        out_shape=jax.ShapeDtypeStruct((B, S, H * MLA_V), BF16),
        compiler_params=pltpu.CompilerParams(dimension_semantics=("arbitrary",) * 2,
                                             vmem_limit_bytes=VMEM_LIMIT_BYTES),
        name="mla_attn",
    )(qm, km, vm)


def _diff_attention(lam, subln_g, qd, kd, vd, out_scale):
    B, H, _, S, _ = qd.shape
    t = min(ATT_T, S)
    return pl.pallas_call(
        functools.partial(_diff_kernel, t=t, out_scale=out_scale),
        grid=(B, S // t),
        in_specs=[
            pl.BlockSpec((1, LANES), lambda b, i: (0, 0)),
            pl.BlockSpec((1, DIFF_DV), lambda b, i: (0, 0)),
            pl.BlockSpec((1, H, 2, t, LANES), lambda b, i: (b, 0, 0, i, 0)),
            pl.BlockSpec((1, H, S, LANES), lambda b, i: (b, 0, 0, 0)),
            pl.BlockSpec((1, H, S, DIFF_DV), lambda b, i: (b, 0, 0, 0)),
        ],
        out_specs=pl.BlockSpec((1, t, H * DIFF_DV), lambda b, i: (b, i, 0)),
        out_shape=jax.ShapeDtypeStruct((B, S, H * DIFF_DV), BF16),
        compiler_params=pltpu.CompilerParams(dimension_semantics=("arbitrary",) * 2,
                                             vmem_limit_bytes=VMEM_LIMIT_BYTES),
        name="diff_attn",
    )(lam, subln_g, qd, kd, vd)


def _topk_rows(vals, k, extra=None, order=None):
    row = lax.broadcasted_iota(jnp.int32, vals.shape, 0) if order is None else order
    n = jnp.iinfo(jnp.int32).max
    out_v, out_i, out_e = [], [], []
    for _ in range(k):
        m = jnp.max(vals, axis=0, keepdims=True)
        sel = jnp.min(jnp.where(vals == m, row, n), axis=0, keepdims=True)
        hit = row == sel
        out_v.append(m)
        out_i.append(sel)
        if extra is not None:
            out_e.append(jnp.max(jnp.where(hit, extra, -1), axis=0, keepdims=True))
        vals = jnp.where(hit, NEG_INF, vals)
    cat = lambda xs: jnp.concatenate(xs, axis=0)
    return cat(out_v), cat(out_i), (cat(out_e) if extra is not None else None)


def _route_kernel(x_ref, mm_ref, md_ref, woa_ref, wob_ref, g1_ref, sc_ref, sh_ref, n2g_ref, wq_ref, keys_ref,
                  x1_ref, h2_ref, eidx_ref, gt_ref, st_scr, sv_scr, si_scr, gt_scr, et_scr):
    x = x_ref[0]
    D = x.shape[-1]
    o = jnp.dot(mm_ref[0], woa_ref[...], preferred_element_type=F32)
    o = o + jnp.dot(md_ref[0], wob_ref[...], preferred_element_type=F32)
    x1 = x + g1_ref[0, 0] * o
    x1_ref[0] = x1
    h2 = _rms(x1, D) * n2g_ref[...] * (1.0 + sc_ref[0, 0]) + sh_ref[0, 0]
    h2_ref[0] = h2
    q = jnp.dot(h2.astype(BF16), wq_ref[...], preferred_element_type=F32).astype(BF16)
    ngroups = 2 * PEER_HEADS
    for g in range(ngroups):
        st_scr[g] = lax.dot_general(keys_ref[g], q[:, g * PEER_DK_HALF:(g + 1) * PEER_DK_HALF], _NT,
                                    preferred_element_type=F32)

    def sub_topk(g2, carry):
        for g in (2 * g2, 2 * g2 + 1):
            v, i, _ = _topk_rows(st_scr[g], PEER_TOPK)
            sv_scr[g] = v
            si_scr[g] = i
        return carry

    lax.fori_loop(0, ngroups // 2, sub_topk, 0)

    def head_topk(hd, carry):
        v0, v1 = sv_scr[2 * hd], sv_scr[2 * hd + 1]
        i0, i1 = si_scr[2 * hd], si_scr[2 * hd + 1]
        k = PEER_TOPK
        ts = v0.shape[1]
        sub = lax.broadcasted_iota(jnp.int32, (8, ts), 0)

        def piece(parts):
            val = jnp.full((8, ts), NEG_INF, F32)
            idx = jnp.zeros((8, ts), jnp.int32)
            pos = jnp.full((8, ts), k * k, jnp.int32)
            for a, r0, cnt in parts:
                v1s = v1[0:8] if r0 == 0 else pltpu.roll(v1[0:8], r0, 0)
                i1s = i1[0:8] if r0 == 0 else pltpu.roll(i1[0:8], r0, 0)
                inside = (sub >= r0) & (sub < r0 + cnt)
                val = jnp.where(inside, v0[a:a + 1] + v1s, val)
                idx = jnp.where(inside, i0[a:a + 1] * PEER_NKEYS + i1s, idx)
                pos = jnp.where(inside, a * k + sub - r0, pos)
            return val, idx, pos

        tiles = [(v0[0:1] + v1[0:8], i0[0:1] * PEER_NKEYS + i1[0:8], sub),
                 (v0[0:1] + v1[8:k], i0[0:1] * PEER_NKEYS + i1[8:k], sub + 8),
                 piece([(1, 0, 8)]), piece([(2, 0, 5), (4, 5, 3)]), piece([(3, 0, 4), (5, 4, 2), (6, 6, 2)]),
                 piece([(7, 0, 2)]),
                 (v0[8:k] + v1[0:1], i0[8:k] * PEER_NKEYS + i1[0:1], (sub + 8) * k)]
        cat = lambda j: jnp.concatenate([t[j] for t in tiles], axis=0)
        fv, _, e = _topk_rows(cat(0), k, extra=cat(1), order=cat(2))
        p = jnp.exp(fv - fv[0:1])
        gate = p / jnp.sum(p, axis=0, keepdims=True)
        r0 = pl.multiple_of(hd * PEER_TOPK, PEER_TOPK)
        et_scr[pl.ds(r0, PEER_TOPK), :] = e
        gt_scr[pl.ds(r0, PEER_TOPK), :] = gate
        return carry

    lax.fori_loop(0, PEER_HEADS, head_topk, 0)
    gt_ref[...] = gt_scr[...]
    eidx_ref[...] = et_scr[...].T


def _route(x, b0, B, mixed_m, mixed_d, woa, wob, mod4, n2g, wq, keys):
    _, S, D = x.shape
    T = B * S
    ts = min(ROUTE_TS, S)
    nst = S // ts
    full = lambda a: pl.BlockSpec(a.shape, lambda b, s: (0,) * a.ndim)
    modspec = lambda k: pl.BlockSpec((1, 1, 1, D), lambda b, s: (b, k, 0, 0))
    tok = lambda w: pl.BlockSpec((1, ts, w), lambda b, s: (b, s, 0))
    return pl.pallas_call(
        _route_kernel,
        grid=(B, nst),
        in_specs=[
            pl.BlockSpec((1, ts, D), lambda b, s: (b0 + b, s, 0)),
            tok(mixed_m.shape[-1]), tok(mixed_d.shape[-1]), full(woa), full(wob),
            modspec(2), modspec(4), modspec(3),
            pl.BlockSpec((1, D), lambda b, s: (0, 0)), full(wq), full(keys),
        ],
        out_specs=[
            tok(D), tok(D),
            pl.BlockSpec((ts, PEER_NE), lambda b, s: (b * nst + s, 0)),
            pl.BlockSpec((PEER_NE, ts), lambda b, s: (0, b * nst + s)),
        ],
        out_shape=[
            jax.ShapeDtypeStruct((B, S, D), F32),
            jax.ShapeDtypeStruct((B, S, D), F32),
            jax.ShapeDtypeStruct((T, PEER_NE), jnp.int32),
            jax.ShapeDtypeStruct((PEER_NE, T), F32),
        ],
        scratch_shapes=[
            pltpu.VMEM((2 * PEER_HEADS, PEER_NKEYS, ts), F32),
            pltpu.VMEM((2 * PEER_HEADS, PEER_TOPK, ts), F32),
            pltpu.VMEM((2 * PEER_HEADS, PEER_TOPK, ts), jnp.int32),
            pltpu.VMEM((PEER_NE, ts), F32),
            pltpu.VMEM((PEER_NE, ts), jnp.int32),
        ],
        compiler_params=pltpu.CompilerParams(dimension_semantics=("arbitrary", "arbitrary"),
                                             vmem_limit_bytes=VMEM_LIMIT_BYTES),
        name="route",
    )(x, mixed_m, mixed_d, woa, wob, mod4, mod4, mod4, n2g, wq, keys)


_SQRT_HALF = float(np.sqrt(0.5))


def _gelu(a):
    return 0.5 * a * (1.0 + lax.erf(a * _SQRT_HALF))


def _mix_token(tile, hrow, gate, before_a=None, before_y=None):
    nchunk = hrow.shape[-1] // LANES
    ngrp = gate.shape[0] // 8
    acts = []
    for g in range(ngrp):
        if before_a is not None:
            before_a(g)
        acc = jnp.zeros((8, LANES), F32)
        for c in range(nchunk):
            u = lax.bitcast_convert_type(tile(g, c) << 16, F32)
            acc = acc + u * hrow[:, c * LANES:(c + 1) * LANES]
        a = jnp.sum(acc, axis=1, keepdims=True)
        acts.append(_gelu(a) * gate[g * 8:(g + 1) * 8])
    yacc = [jnp.zeros((8, LANES), F32) for _ in range(nchunk)]
    for g in range(ngrp):
        if before_y is not None:
            before_y(g)
        for c in range(nchunk):
            v = lax.bitcast_convert_type(tile(g, c) & jnp.uint32(0xFFFF0000), F32)
            yacc[c] = yacc[c] + v * acts[g]
    return jnp.concatenate([jnp.sum(yc, axis=0, keepdims=True) for yc in yacc], axis=1)


def _gate_column(gt_ref, token):
    lane = lax.broadcasted_iota(jnp.int32, gt_ref.shape, 1)
    return jnp.sum(jnp.where(lane == token, gt_ref[...], 0.0), axis=1, keepdims=True)


def _peer_kernel(idx_ref, idxn_ref, h_ref, gt_ref, x1_ref, g2_ref, tab_ref, o_ref, buf, sem, *, tt):
    ne = PEER_NE
    rt = PEER_ROW_TILE
    look = PEER_NSLOT - 1
    nchunk = h_ref.shape[-1] // LANES
    ngrp = ne // 8
    step = pl.program_id(0)
    nsteps = pl.num_programs(0)
    nparts = 2 * ngrp
    per = ne // nparts

    def issue_part(iref, t, slot, part):
        for e in range(part * per, (part + 1) * per):
            row = pl.multiple_of(iref[t, e] * rt, rt)
            g, r = divmod(e, 8)
            pltpu.make_async_copy(tab_ref.at[pl.ds(row, rt), :], buf.at[slot, pl.ds(g * nchunk, nchunk), r, :],
                                  sem.at[slot]).start(priority=e % 2)

    def wait(slot):
        pltpu.make_async_copy(buf.at[slot], buf.at[slot], sem.at[slot]).wait()

    @pl.when(step == 0)
    def _():
        for t in range(look):
            for part in range(nparts):
                issue_part(idx_ref, t, t, part)

    def group(base, nxt_of):
        h8 = h_ref[pl.ds(base, 8), :]
        ys = []
        for j in range(8):
            iref, tn = nxt_of(j)
            nslot = (j + look) % PEER_NSLOT
            slot = j % PEER_NSLOT
            wait(slot)
            ys.append(_mix_token(lambda g, c: buf[slot, g * nchunk + c], h8[j:j + 1], _gate_column(gt_ref, base + j),
                                 before_a=lambda g: issue_part(iref, tn, nslot, g),
                                 before_y=lambda g: issue_part(iref, tn, nslot, ngrp + g)))
        y = jnp.concatenate(ys, axis=0)
        o_ref[pl.ds(base, 8), :] = x1_ref[pl.ds(base, 8), :] + g2_ref[0, 0] * y

    def body(gi, carry):
        base = pl.multiple_of(gi * 8, 8)
        group(base, lambda j: (idx_ref, base + j + look))
        return carry

    lax.fori_loop(0, tt // 8 - 1, body, 0)
    last = tt - 8
    group(last, lambda j: (idx_ref, last + j + look) if j + look < 8 else (idxn_ref, j + look - 8))

    @pl.when(step == nsteps - 1)
    def _():
        for t in range(look):
            wait(t)


def _peer_gathered(eidx, h2, gt, x1, mod4, table, nblk, S):
    T, D = h2.shape
    tt = min(PEER_TT, S)
    nst = S // tt
    nchunk = D // LANES
    return pl.pallas_call(
        functools.partial(_peer_kernel, tt=tt),
        grid=(nblk,),
        in_specs=[
            pl.BlockSpec((tt, PEER_NE), lambda i: (i, 0), memory_space=pltpu.SMEM),
            pl.BlockSpec((tt, PEER_NE), lambda i: (jnp.minimum(i + 1, nblk - 1), 0), memory_space=pltpu.SMEM),
            pl.BlockSpec((tt, D), lambda i: (i, 0)),
            pl.BlockSpec((PEER_NE, tt), lambda i: (0, i)),
            pl.BlockSpec((tt, D), lambda i: (i, 0)),
            pl.BlockSpec((1, 1, 1, D), lambda i: (i // nst, 5, 0, 0)),
            pl.BlockSpec(memory_space=pl.ANY),
        ],
        out_specs=pl.BlockSpec((tt, D), lambda i: (i, 0)),
        out_shape=jax.ShapeDtypeStruct((T, D), F32),
        scratch_shapes=[pltpu.VMEM((PEER_NSLOT, PEER_NE // 8 * nchunk, 8, LANES), jnp.uint32),
                        pltpu.SemaphoreType.DMA((PEER_NSLOT,))],
        compiler_params=pltpu.CompilerParams(dimension_semantics=("arbitrary",),
                                             vmem_limit_bytes=VMEM_LIMIT_BYTES),
        name="peer_gather",
    )(eidx, eidx, h2, gt, x1, mod4, table)


def _sc_gather(table, idx):
    nb = idx.shape[0]
    d = table.shape[1]
    per_w = nb // SC_WORKERS
    nchunks = per_w // SC_CHUNK
    mesh = plsc.VectorSubcoreMesh(core_axis_name="c", subcore_axis_name="s")

    @functools.partial(
        pl.kernel, mesh=mesh,
        out_type=jax.ShapeDtypeStruct((nb, d), table.dtype),
        scratch_types=[pltpu.VMEM((SC_CHUNK,), jnp.int32), pltpu.VMEM((SC_CHUNK, d), table.dtype),
                       pltpu.SemaphoreType.DMA],
    )
    def k(table_hbm, idx_hbm, out_hbm, idx_v, rows_v, sem):
        wid = lax.axis_index("s") * 2 + lax.axis_index("c")
        base = wid * per_w

        @pl.loop(0, nchunks)
        def _(i):
            off = pl.multiple_of(base + i * SC_CHUNK, SC_CHUNK)
            pltpu.sync_copy(idx_hbm.at[pl.ds(off, SC_CHUNK)], idx_v)
            pltpu.async_copy(table_hbm.at[idx_v], rows_v, sem).wait()
            pltpu.sync_copy(rows_v, out_hbm.at[pl.ds(off, SC_CHUNK)])

    return k(table, idx)


def _dense_kernel(rows_ref, h_ref, gt_ref, x1_ref, g2_ref, prev_ref, o_ref):
    ne = PEER_NE
    nd = h_ref.shape[0]
    i = pl.program_id(0)
    per_blk = gt_ref.shape[1] // nd
    ys = []
    for j in range(nd):
        tile = lambda g, c, j=j: rows_ref[j * ne + g * 8:j * ne + (g + 1) * 8, c * LANES:(c + 1) * LANES]
        ys.append(_mix_token(tile, h_ref[j:j + 1, :], _gate_column(gt_ref, (i % per_blk) * nd + j)))
    o_ref[...] = x1_ref[...] + g2_ref[0, 0] * jnp.concatenate(ys, axis=0)


def _peer_dense(rows, h2, gt, x1, mod4, out, blk0, nblk, S):
    T, D = h2.shape
    tt = min(PEER_TT, S)
    nst = S // tt
    nd = DENSE_TOK
    per_blk = tt // nd
    return pl.pallas_call(
        _dense_kernel,
        grid=(nblk * per_blk,),
        in_specs=[
            pl.BlockSpec((nd * PEER_NE, D), lambda i: (i, 0)),
            pl.BlockSpec((nd, D), lambda i: (blk0 * per_blk + i, 0)),
            pl.BlockSpec((PEER_NE, tt), lambda i: (0, blk0 + i // per_blk)),
            pl.BlockSpec((nd, D), lambda i: (blk0 * per_blk + i, 0)),
            pl.BlockSpec((1, 1, 1, D), lambda i: ((blk0 + i // per_blk) // nst, 5, 0, 0)),
            pl.BlockSpec(memory_space=pl.ANY),
        ],
        out_specs=pl.BlockSpec((nd, D), lambda i: (blk0 * per_blk + i, 0)),
        out_shape=jax.ShapeDtypeStruct((T, D), F32),
        input_output_aliases={5: 0},
        compiler_params=pltpu.CompilerParams(dimension_semantics=("arbitrary",),
                                             vmem_limit_bytes=VMEM_LIMIT_BYTES),
        name="peer_dense",
    )(rows, h2, gt, x1, mod4, out)


def _rope_tables(S):
    half = MLA_ROPE // 2
    inv = 1.0 / (ROPE_THETA ** (jnp.arange(half, dtype=F32) / half))
    ang = jnp.arange(S, dtype=F32)[:, None] * inv[None, :]
    cos, sin = jnp.cos(ang), jnp.sin(ang)
    cosf = jnp.tile(cos, (1, 4))
    sinf = jnp.concatenate([-sin, sin, -sin, sin], axis=1)
    return cosf, sinf


def _pad_lanes(v, fill):
    return jnp.concatenate([v, jnp.full((LANES - v.shape[0],), fill, v.dtype)]).reshape(1, LANES)


def _pack_table(u, v):
    ub = lax.bitcast_convert_type(u.astype(BF16), jnp.uint16).astype(jnp.uint32)
    vb = lax.bitcast_convert_type(v.astype(BF16), jnp.uint16).astype(jnp.uint32)
    return ub | (vb << 16)


def kernel(x, c, ada_w, ada_b, norm1_g, w_in, mla_q_lat_g, mla_w_q_up, mla_kv_lat_g, mla_w_kv_up, mla_q_g, mla_k_g, diff_q_g, diff_k_g, diff_lq1, diff_lk1, diff_lq2, diff_lk2, diff_subln_g, w_out, norm2_g, peer_w_q, peer_sub_keys, peer_u, peer_v):
    B, S, D = x.shape
    depth = ada_w.shape[0]
    cosf, sinf = _rope_tables(S)
    o0 = MLA_Q_LORA
    o1 = o0 + MLA_KV_LORA
    o2 = o1 + MLA_ROPE
    nq = DIFF_HEADS * 2 * DIFF_DQK
    o3 = o2 + nq
    o4 = o3 + nq
    for l in range(depth):
        lambda_init = 0.8 - 0.6 * math.exp(-0.3 * l)
        mod, lam = _ada(c, ada_w[l], ada_b[l], diff_lq1[l], diff_lk1[l], diff_lq2[l], diff_lk2[l], lambda_init)
        mod4 = mod.reshape(B, 6, 1, D)

        wi = w_in[l]
        win_p = jnp.concatenate([wi[:, :o1], wi[:, o2:], wi[:, o1:o2], jnp.zeros((D, LANES - MLA_ROPE), F32)],
                                axis=1).astype(BF16)
        wq3 = mla_w_q_up[l].reshape(MLA_Q_LORA, MLA_HEADS, MLA_QK)
        wq_p = jnp.concatenate([wq3, jnp.zeros((MLA_Q_LORA, MLA_HEADS, 256 - MLA_QK), F32)], axis=2)
        wq_p = wq_p.reshape(MLA_Q_LORA, MLA_HEADS * 256).astype(BF16)
        wkv = mla_w_kv_up[l].astype(BF16)
        wo = w_out[l].astype(BF16)
        nm = MLA_HEADS * MLA_V
        keys = peer_sub_keys[l].reshape(2 * PEER_HEADS, PEER_NKEYS, PEER_DK_HALF).astype(BF16)
        wpq = peer_w_q[l].astype(BF16)
        table = _pack_table(peer_u[l], peer_v[l])
        tab_tc = table.reshape(-1, LANES)

        def routed(b0, nb, m4):
            qm, km, vm, qd, kd, vd = _proj(
                x, b0, nb, m4, norm1_g[l].reshape(1, D), win_p, mla_q_lat_g[l].reshape(1, -1), wq_p,
                mla_kv_lat_g[l].reshape(1, -1), wkv,
                mla_q_g[l, :MLA_NOPE].reshape(1, -1), _pad_lanes(mla_q_g[l, MLA_NOPE:], 1.0),
                mla_k_g[l, :MLA_NOPE].reshape(1, -1), _pad_lanes(mla_k_g[l, MLA_NOPE:], 1.0),
                jnp.tile(diff_q_g[l], 2).reshape(1, -1), jnp.tile(diff_k_g[l], 2).reshape(1, -1), cosf, sinf)
            mixed_m = _mla_attention(qm, km, vm)
            mixed_d = _diff_attention(lam, diff_subln_g[l].reshape(1, -1), qd, kd, vd, 1.0 - lambda_init)
            return _route(x, b0, nb, mixed_m, mixed_d, wo[:nm], wo[nm:], m4, norm2_g[l].reshape(1, D), wpq, keys)

        groups = SEQ_GROUPS if sum(g[0] for g in SEQ_GROUPS) == B else ((B,) + SEQ_GROUPS[0][1:],)
        tt = min(PEER_TT, S)
        starts = [sum(g[0] for g in groups[:i]) for i in range(len(groups))]
        m4s = [mod4[b0:b0 + g[0]] for b0, g in zip(starts, groups)]
        rts = [routed(b0, g[0], m4) for b0, g, m4 in zip(starts, groups, m4s)]
        outs = []
        for (nb, num, den), m4, (x1, h2, eidx, gt) in zip(groups, m4s, rts):
            nblk = nb * S // tt
            nsc = (nblk * num // den) if nblk >= den else 0
            ntc = nblk - nsc
            h2f, x1f = h2.reshape(nb * S, D), x1.reshape(nb * S, D)
            rows = _sc_gather(table, eidx[ntc * tt:].reshape(-1)) if nsc else None
            out = _peer_gathered(eidx, h2f, gt, x1f, m4, tab_tc, ntc, S)
            if nsc:
                out = _peer_dense(rows, h2f, gt, x1f, m4, out, ntc, nsc, S)
            outs.append(out.reshape(nb, S, D))
        x = jnp.concatenate(outs, axis=0) if len(outs) > 1 else outs[0]
    return x
```
